```python
import jax, jax.numpy as jnp
from jax import lax
import numpy as np

D_MODEL = 4096
BATCH = 1
SEQ = 8192
DEPTH = 4

EXPAND = 2
D_INNER = EXPAND * D_MODEL
HEAD_DIM = 128
N_HEADS = D_INNER // HEAD_DIM
GROUP_WIDTH = 128
N_GROUPS = D_INNER // GROUP_WIDTH
CHUNK = 128
Q_BLOCK = 128
CONV_WIDTH = 3
N_MIXERS = 3
RMS_EPS = 1e-6
NEG_INF = -1e30

kernel_name = "interleaved_sgu_fox_shortconv_trunk"


def rmsnorm(x, gain):
    xf = x.astype(jnp.float32)
    y = xf * lax.rsqrt(jnp.mean(xf * xf, axis=-1, keepdims=True) + RMS_EPS)
    return (y * gain.astype(jnp.float32)).astype(x.dtype)


def mixer_a(h, w_in, v_gain, ws, ws_bias):
    b, l, _ = h.shape
    u, v, g = jnp.split(h @ w_in, 3, axis=-1)
    v = rmsnorm(v, v_gain)
    v = v.reshape(b, l // CHUNK, CHUNK, N_GROUPS, GROUP_WIDTH)
    causal = jnp.tril(jnp.ones((CHUNK, CHUNK), dtype=bool))
    ws_c = jnp.where(causal[None], ws, jnp.zeros((), ws.dtype)).astype(v.dtype)
    s = jnp.einsum('gts,bnsgc->bntgc', ws_c, v) + ws_bias.T.astype(v.dtype)[:, :, None]
    s = s.reshape(b, l, D_INNER)
    return u * s * jax.nn.silu(g)


def mixer_b(h, w_in, b_f):
    b, l, _ = h.shape
    z = h @ w_in
    q, k, v, g = jnp.split(z[..., :4 * D_INNER], 4, axis=-1)
    f_logit = z[..., 4 * D_INNER:]
    log_f = jax.nn.log_sigmoid(f_logit.astype(jnp.float32) + b_f.astype(jnp.float32))
    cum = jnp.swapaxes(jnp.cumsum(log_f, axis=1), 1, 2)
    q = q.reshape(b, l, N_HEADS, HEAD_DIM)
    k = k.reshape(b, l, N_HEADS, HEAD_DIM)
    v = v.reshape(b, l, N_HEADS, HEAD_DIM)
    scale = HEAD_DIM ** -0.5
    key_pos = jnp.arange(l)

    def attend_block(i):
        start = i * Q_BLOCK
        qb = lax.dynamic_slice_in_dim(q, start, Q_BLOCK, axis=1)
        cq = lax.dynamic_slice_in_dim(cum, start, Q_BLOCK, axis=2)
        s = jnp.einsum('bqhd,bkhd->bhqk', qb, k).astype(jnp.float32) * scale
        s = s + cq[..., :, None] - cum[..., None, :]
        q_pos = start + jnp.arange(Q_BLOCK)
        s = jnp.where(q_pos[:, None] >= key_pos[None, :], s, NEG_INF)
        p = jax.nn.softmax(s, axis=-1).astype(v.dtype)
        return jnp.einsum('bhqk,bkhd->bqhd', p, v)

    o = lax.map(attend_block, jnp.arange(l // Q_BLOCK))
    o = jnp.moveaxis(o, 0, 1).reshape(b, l, D_INNER)
    return o * jax.nn.silu(g)


def mixer_c(h, w_in, conv_w):
    bg, cg, hh, g = jnp.split(h @ w_in, 4, axis=-1)
    inner = cg * hh
    conv = lax.conv_general_dilated(
        inner, conv_w[:, None, :].astype(inner.dtype),
        window_strides=(1,), padding=((CONV_WIDTH - 1, 0),),
        dimension_numbers=('NWC', 'WIO', 'NWC'), feature_group_count=D_INNER)
    return bg * conv * jax.nn.silu(g)


def setup_inputs(seed: int = 0) -> dict:
    key = jax.random.key(seed)
    keys = jax.random.split(key, 40)
    f32 = jnp.float32
    d_s = D_MODEL ** -0.5
    e_s = D_INNER ** -0.5

    def gain(k, n):
        return 1.0 + 0.02 * jax.random.normal(k, (n,), f32)

    def layer_a(ks):
        return (gain(ks[0], D_MODEL),
                jax.random.normal(ks[1], (D_MODEL, 3 * D_INNER), f32) * d_s,
                gain(ks[2], D_INNER),
                jax.random.normal(ks[3], (N_GROUPS, CHUNK, CHUNK), f32) * CHUNK ** -0.5,
                1.0 + 0.02 * jax.random.normal(ks[4], (N_GROUPS, CHUNK), f32),
                jax.random.normal(ks[5], (D_INNER, D_MODEL), f32) * e_s)

    l0 = layer_a(keys[0:6])
    l3 = layer_a(keys[6:12])
    l1_norm = gain(keys[12], D_MODEL)
    l1_w_in = jax.random.normal(keys[13], (D_MODEL, 4 * D_INNER + N_HEADS), f32) * d_s
    l1_b_f = jnp.linspace(2.0, 7.0, N_HEADS, dtype=f32) + 0.01 * jax.random.normal(keys[14], (N_HEADS,), f32)
    l1_w_out = jax.random.normal(keys[15], (D_INNER, D_MODEL), f32) * e_s
    l2_norm = gain(keys[16], D_MODEL)
    l2_w_in = jax.random.normal(keys[17], (D_MODEL, 4 * D_INNER), f32) * d_s
    l2_conv_w = jax.random.normal(keys[18], (CONV_WIDTH, D_INNER), f32) * CONV_WIDTH ** -0.5
    l2_w_out = jax.random.normal(keys[19], (D_INNER, D_MODEL), f32) * e_s
    x = jax.random.normal(keys[20], (BATCH, SEQ, D_MODEL), f32)
    final_norm = gain(keys[21], D_MODEL)
    return {
        "x": x,
        "l0_norm": l0[0], "l0_w_in": l0[1], "l0_v_gain": l0[2], "l0_ws": l0[3], "l0_ws_bias": l0[4], "l0_w_out": l0[5],
        "l1_norm": l1_norm, "l1_w_in": l1_w_in, "l1_b_f": l1_b_f, "l1_w_out": l1_w_out,
        "l2_norm": l2_norm, "l2_w_in": l2_w_in, "l2_conv_w": l2_conv_w, "l2_w_out": l2_w_out,
        "l3_norm": l3[0], "l3_w_in": l3[1], "l3_v_gain": l3[2], "l3_ws": l3[3], "l3_ws_bias": l3[4], "l3_w_out": l3[5],
        "final_norm": final_norm,
    }


def reference(x,
              l0_norm, l0_w_in, l0_v_gain, l0_ws, l0_ws_bias, l0_w_out,
              l1_norm, l1_w_in, l1_b_f, l1_w_out,
              l2_norm, l2_w_in, l2_conv_w, l2_w_out,
              l3_norm, l3_w_in, l3_v_gain, l3_ws, l3_ws_bias, l3_w_out,
              final_norm):
    layers = (
        (l0_norm, l0_w_in, (l0_v_gain, l0_ws, l0_ws_bias), l0_w_out),
        (l1_norm, l1_w_in, (l1_b_f,), l1_w_out),
        (l2_norm, l2_w_in, (l2_conv_w,), l2_w_out),
        (l3_norm, l3_w_in, (l3_v_gain, l3_ws, l3_ws_bias), l3_w_out),
    )
    mixers = (mixer_a, mixer_b, mixer_c)
    for i in range(DEPTH):
        norm_g, w_in, extra, w_out = layers[i]
        y = mixers[i % N_MIXERS](rmsnorm(x, norm_g), w_in, *extra)
        x = x + y @ w_out
    return rmsnorm(x, final_norm)
```

```python
import functools

import jax
import jax.numpy as jnp
from jax import lax
from jax.experimental import pallas as pl
from jax.experimental.pallas import tpu as pltpu

F32 = jnp.float32
BF16 = jnp.bfloat16

SEQ = 8192
D_MODEL = 4096
D_INNER = 8192
HEAD_DIM = 128
N_HEADS = D_INNER // HEAD_DIM
GROUP_WIDTH = 128
N_GROUPS = D_INNER // GROUP_WIDTH
CHUNK = 128
CONV_WIDTH = 3
RMS_EPS = 1e-6
NEG_INF = -1e30

LANES = 128
SUBLANES = 8
MIB = 1024 * 1024

PROJ_TM = 1024
PROJ_TN = 256
OUT_TM = 1024
OUT_TN = 1024
OUT_TK = 2048
ATT_TQ = 512
ATT_TK = 512
CUM_ROWS = 256
NORM_ROWS = 256


def _params(semantics, vmem_mib):
    return pltpu.CompilerParams(dimension_semantics=semantics, vmem_limit_bytes=vmem_mib * MIB)


def _silu(x):
    return x * jax.nn.sigmoid(x)


def _rmsnorm_kernel(x_ref, g_ref, o_ref):
    x = x_ref[...]
    r = lax.rsqrt(jnp.mean(x * x, axis=-1, keepdims=True) + RMS_EPS)
    o_ref[...] = (x * r * g_ref[...]).astype(o_ref.dtype)


def _rmsnorm(x, gain, out_dtype):
    rows, width = x.shape
    return pl.pallas_call(
        _rmsnorm_kernel,
        grid=(rows // NORM_ROWS,),
        in_specs=[pl.BlockSpec((NORM_ROWS, width), lambda i: (i, 0)),
                  pl.BlockSpec((1, width), lambda i: (0, 0))],
        out_specs=pl.BlockSpec((NORM_ROWS, width), lambda i: (i, 0)),
        out_shape=jax.ShapeDtypeStruct((rows, width), out_dtype),
        compiler_params=_params(("parallel",), 32),
        name="rmsnorm",
    )(x, gain.reshape(1, width))


def _weight_specs(n_split):
    blocks_per_split = D_INNER // PROJ_TN
    return [pl.BlockSpec((D_MODEL, PROJ_TN), functools.partial(
        lambda j, m, s: (0, s * blocks_per_split + j), s=s)) for s in range(n_split)]


def _h_spec():
    return pl.BlockSpec((PROJ_TM, D_MODEL), lambda j, m: (m, 0))


def _row_tile_spec():
    return pl.BlockSpec((PROJ_TM, PROJ_TN), lambda j, m: (m, j))


def _col_tile_spec():
    return pl.BlockSpec((PROJ_TN, PROJ_TM), lambda j, m: (j, m))


def _proj_grid():
    return (D_INNER // PROJ_TN, SEQ // PROJ_TM)


def _dot(a, b):
    return jnp.dot(a, b, preferred_element_type=F32)


def _proj_a_kernel(h_ref, wu_ref, wv_ref, wg_ref, ug_ref, v_ref):
    h = h_ref[...]
    u = _dot(h, wu_ref[...])
    g = _dot(h, wg_ref[...])
    ug_ref[...] = u * _silu(g)
    v_ref[...] = _dot(h, wv_ref[...])


def _proj_a(h, w_in):
    return pl.pallas_call(
        _proj_a_kernel,
        grid=_proj_grid(),
        in_specs=[_h_spec()] + _weight_specs(3),
        out_specs=[_row_tile_spec(), _row_tile_spec()],
        out_shape=[jax.ShapeDtypeStruct((SEQ, D_INNER), F32),
                   jax.ShapeDtypeStruct((SEQ, D_INNER), F32)],
        compiler_params=_params(("parallel", "parallel"), 48),
        name="proj_a",
    )(h, w_in, w_in, w_in)


def _proj_b_kernel(h_ref, wq_ref, wk_ref, wv_ref, wg_ref, qt_ref, k_ref, vt_ref, sg_ref):
    h = h_ref[...]
    q = _dot(h, wq_ref[...]) * (HEAD_DIM ** -0.5)
    qt_ref[...] = q.T.astype(BF16)
    k_ref[...] = _dot(h, wk_ref[...]).astype(BF16)
    vt_ref[...] = _dot(h, wv_ref[...]).T.astype(BF16)
    sg_ref[...] = _silu(_dot(h, wg_ref[...]))


def _proj_b(h, w_in):
    return pl.pallas_call(
        _proj_b_kernel,
        grid=_proj_grid(),
        in_specs=[_h_spec()] + _weight_specs(4),
        out_specs=[_col_tile_spec(), _row_tile_spec(), _col_tile_spec(), _row_tile_spec()],
        out_shape=[jax.ShapeDtypeStruct((D_INNER, SEQ), BF16),
                   jax.ShapeDtypeStruct((SEQ, D_INNER), BF16),
                   jax.ShapeDtypeStruct((D_INNER, SEQ), BF16),
                   jax.ShapeDtypeStruct((SEQ, D_INNER), F32)],
        compiler_params=_params(("parallel", "parallel"), 56),
        name="proj_b",
    )(h, w_in, w_in, w_in, w_in)


def _proj_c_kernel(h_ref, wb_ref, wc_ref, wh_ref, wg_ref, cw_ref, y_ref, carry_ref):
    m = pl.program_id(1)

    @pl.when(m == 0)
    def _():
        carry_ref[...] = jnp.zeros_like(carry_ref)

    h = h_ref[...]
    inner = _dot(h, wc_ref[...]) * _dot(h, wh_ref[...])
    prev = carry_ref[...]
    p1 = prev[SUBLANES - 1:SUBLANES, :]
    p2 = prev[SUBLANES - 2:SUBLANES - 1, :]
    row = lax.broadcasted_iota(jnp.int32, inner.shape, 0)
    d1 = jnp.where(row == 0, p1, pltpu.roll(inner, 1, 0))
    d2 = jnp.where(row == 0, p2, jnp.where(row == 1, p1, pltpu.roll(inner, 2, 0)))
    cw = cw_ref[...]
    conv = cw[0:1, :] * d2 + cw[1:2, :] * d1 + cw[2:3, :] * inner
    carry_ref[...] = inner[PROJ_TM - SUBLANES:, :]
    bg = _dot(h, wb_ref[...])
    g = _dot(h, wg_ref[...])
    y_ref[...] = (bg * conv * _silu(g)).astype(y_ref.dtype)


def _proj_c(h, w_in, conv_w):
    return pl.pallas_call(
        _proj_c_kernel,
        grid=_proj_grid(),
        in_specs=[_h_spec()] + _weight_specs(4)
        + [pl.BlockSpec((CONV_WIDTH, PROJ_TN), lambda j, m: (0, j))],
        out_specs=_row_tile_spec(),
        out_shape=jax.ShapeDtypeStruct((SEQ, D_INNER), BF16),
        scratch_shapes=[pltpu.VMEM((SUBLANES, PROJ_TN), F32)],
        compiler_params=_params(("parallel", "arbitrary"), 56),
        name="proj_c",
    )(h, w_in, w_in, w_in, w_in, conv_w)


def _sgu_kernel(v_ref, ug_ref, gain_ref, ws_ref, bias_ref, y_ref, wsc_ref):
    @pl.when(pl.program_id(0) == 0)
    def _():
        t = lax.broadcasted_iota(jnp.int32, (CHUNK, CHUNK), 0)
        s = lax.broadcasted_iota(jnp.int32, (CHUNK, CHUNK), 1)
        for g in range(N_GROUPS):
            wsc_ref[g] = jnp.where(t >= s, ws_ref[g], 0.0).astype(BF16)

    v = v_ref[...]
    r = lax.rsqrt(jnp.mean(v * v, axis=-1, keepdims=True) + RMS_EPS)
    for g in range(N_GROUPS):
        cols = slice(g * GROUP_WIDTH, (g + 1) * GROUP_WIDTH)
        vn = (v_ref[:, cols] * r * gain_ref[:, cols]).astype(BF16)
        s = _dot(wsc_ref[g], vn) + bias_ref[g]
        y_ref[:, cols] = (ug_ref[:, cols] * s).astype(y_ref.dtype)


def _sgu(v, ug, v_gain, ws, ws_bias):
    bias_b = jnp.broadcast_to(ws_bias[:, :, None], (N_GROUPS, CHUNK, GROUP_WIDTH))
    return pl.pallas_call(
        _sgu_kernel,
        grid=(SEQ // CHUNK,),
        in_specs=[pl.BlockSpec((CHUNK, D_INNER), lambda n: (n, 0)),
                  pl.BlockSpec((CHUNK, D_INNER), lambda n: (n, 0)),
                  pl.BlockSpec((1, D_INNER), lambda n: (0, 0)),
                  pl.BlockSpec((N_GROUPS, CHUNK, CHUNK), lambda n: (0, 0, 0)),
                  pl.BlockSpec((N_GROUPS, CHUNK, GROUP_WIDTH), lambda n: (0, 0, 0))],
        out_specs=pl.BlockSpec((CHUNK, D_INNER), lambda n: (n, 0)),
        out_shape=jax.ShapeDtypeStruct((SEQ, D_INNER), BF16),
        scratch_shapes=[pltpu.VMEM((N_GROUPS, CHUNK, CHUNK), BF16)],
        compiler_params=_params(("arbitrary",), 48),
        name="sgu",
    )(v, ug, v_gain.reshape(1, D_INNER), ws, bias_b)


def _forget_cumsum_kernel(h_ref, wf_ref, bf_ref, cum_ref, cumb_ref, carry_ref):
    @pl.when(pl.program_id(0) == 0)
    def _():
        carry_ref[...] = jnp.zeros_like(carry_ref)

    x = _dot(h_ref[...], wf_ref[...]) + bf_ref[...]
    log_f = jnp.minimum(x, 0.0) - jnp.log1p(jnp.exp(-jnp.abs(x)))
    t = lax.broadcasted_iota(jnp.int32, (CUM_ROWS, CUM_ROWS), 0)
    s = lax.broadcasted_iota(jnp.int32, (CUM_ROWS, CUM_ROWS), 1)
    tril = (t >= s).astype(F32)
    cum = jnp.dot(tril, log_f, preferred_element_type=F32,
                  precision=lax.Precision.HIGHEST) + carry_ref[...]
    carry_ref[...] = cum[CUM_ROWS - 1:CUM_ROWS, :]
    cum_ref[...] = cum.T
    for hd in range(N_HEADS):
        cumb_ref[hd] = jnp.broadcast_to(cum[:, hd:hd + 1], (CUM_ROWS, LANES))


def _forget_cumsum(h, w_f, b_f):
    return pl.pallas_call(
        _forget_cumsum_kernel,
        grid=(SEQ // CUM_ROWS,),
        in_specs=[pl.BlockSpec((CUM_ROWS, D_MODEL), lambda i: (i, 0)),
                  pl.BlockSpec((D_MODEL, LANES), lambda i: (0, 0)),
                  pl.BlockSpec((1, LANES), lambda i: (0, 0))],
        out_specs=[pl.BlockSpec((LANES, CUM_ROWS), lambda i: (0, i)),
                   pl.BlockSpec((N_HEADS, CUM_ROWS, LANES), lambda i: (0, i, 0))],
        out_shape=[jax.ShapeDtypeStruct((LANES, SEQ), F32),
                   jax.ShapeDtypeStruct((N_HEADS, SEQ, LANES), F32)],
        scratch_shapes=[pltpu.VMEM((1, LANES), F32)],
        compiler_params=_params(("arbitrary",), 48),
        name="forget_cumsum",
    )(h, w_f, b_f)


def _attn_kernel(qt_ref, k_ref, vt_ref, cumb_ref, cq_ref, sg_ref, y_ref):
    i = pl.program_id(1)
    qt = qt_ref[...]
    cq = cq_ref[...]
    lane_tiles = ATT_TQ // LANES

    def step(j, carry, masked):
        m, l, acc = carry
        ks = pl.multiple_of(j * ATT_TK, ATT_TK)
        st = _dot(k_ref[pl.ds(ks, ATT_TK), :], qt)
        ck = cumb_ref[pl.ds(ks, ATT_TK), :]
        st = st + (cq - jnp.concatenate([ck] * lane_tiles, axis=1))
        if masked:
            kpos = ks + lax.broadcasted_iota(jnp.int32, st.shape, 0)
            qpos = i * ATT_TQ + lax.broadcasted_iota(jnp.int32, st.shape, 1)
            st = jnp.where(qpos >= kpos, st, NEG_INF)
        m_new = jnp.maximum(m, jnp.max(st, axis=0, keepdims=True))
        alpha = jnp.exp(m - m_new)
        p = jnp.exp(st - m_new)
        l = alpha * l + jnp.sum(p, axis=0, keepdims=True)
        acc = alpha * acc + _dot(vt_ref[:, pl.ds(ks, ATT_TK)], p.astype(BF16))
        return m_new, l, acc

    init = (jnp.full((1, ATT_TQ), NEG_INF, F32), jnp.zeros((1, ATT_TQ), F32),
            jnp.zeros((HEAD_DIM, ATT_TQ), F32))
    carry = lax.fori_loop(0, i, lambda j, c: step(j, c, False), init)
    _, l, acc = step(i, carry, True)
    o = (acc / l).T
    y_ref[...] = (o * sg_ref[...]).astype(y_ref.dtype)


def _attention(qt, k, vt, cumb, cum_t, sg):
    return pl.pallas_call(
        _attn_kernel,
        grid=(N_HEADS, SEQ // ATT_TQ),
        in_specs=[pl.BlockSpec((HEAD_DIM, ATT_TQ), lambda h, i: (h, i)),
                  pl.BlockSpec((SEQ, HEAD_DIM), lambda h, i: (0, h)),
                  pl.BlockSpec((HEAD_DIM, SEQ), lambda h, i: (h, 0)),
                  pl.BlockSpec((None, SEQ, LANES), lambda h, i: (h, 0, 0)),
                  pl.BlockSpec((None, 1, ATT_TQ), lambda h, i: (h, 0, i)),
                  pl.BlockSpec((ATT_TQ, HEAD_DIM), lambda h, i: (i, h))],
        out_specs=pl.BlockSpec((ATT_TQ, HEAD_DIM), lambda h, i: (i, h)),
        out_shape=jax.ShapeDtypeStruct((SEQ, D_INNER), BF16),
        compiler_params=_params(("parallel", "arbitrary"), 48),
        name="fox_attention",
    )(qt, k, vt, cumb, cum_t.reshape(LANES, 1, SEQ), sg)


def _out_proj_kernel(y_ref, w_ref, x_ref, o_ref):
    kk = pl.program_id(2)
    part = _dot(y_ref[...], w_ref[...])

    @pl.when(kk == 0)
    def _():
        o_ref[...] = x_ref[...] + part

    @pl.when(kk != 0)
    def _():
        o_ref[...] += part


def _out_proj(y, w_out, x):
    return pl.pallas_call(
        _out_proj_kernel,
        grid=(SEQ // OUT_TM, D_MODEL // OUT_TN, D_INNER // OUT_TK),
        in_specs=[pl.BlockSpec((OUT_TM, OUT_TK), lambda m, n, k: (m, k)),
                  pl.BlockSpec((OUT_TK, OUT_TN), lambda m, n, k: (k, n)),
                  pl.BlockSpec((OUT_TM, OUT_TN), lambda m, n, k: (m, n))],
        out_specs=pl.BlockSpec((OUT_TM, OUT_TN), lambda m, n, k: (m, n)),
        out_shape=jax.ShapeDtypeStruct((SEQ, D_MODEL), F32),
        compiler_params=_params(("parallel", "parallel", "arbitrary"), 48),
        name="out_proj",
    )(y, w_out, x)


def _layer_a(x, norm_g, w_in, v_gain, ws, ws_bias, w_out):
    h = _rmsnorm(x, norm_g, BF16)
    ug, v = _proj_a(h, w_in.astype(BF16))
    y = _sgu(v, ug, v_gain, ws, ws_bias)
    return _out_proj(y, w_out.astype(BF16), x)


def _layer_b(x, norm_g, w_in, b_f, w_out):
    h = _rmsnorm(x, norm_g, BF16)
    w_bf = w_in.astype(BF16)
    pad = LANES - N_HEADS
    w_f = jnp.pad(w_bf[:, 4 * D_INNER:], ((0, 0), (0, pad)))
    b_pad = jnp.pad(b_f, (0, pad)).reshape(1, LANES)
    qt, k, vt, sg = _proj_b(h, w_bf)
    cum_t, cumb = _forget_cumsum(h, w_f, b_pad)
    y = _attention(qt, k, vt, cumb, cum_t, sg)
    return _out_proj(y, w_out.astype(BF16), x)


def _layer_c(x, norm_g, w_in, conv_w, w_out):
    h = _rmsnorm(x, norm_g, BF16)
    y = _proj_c(h, w_in.astype(BF16), conv_w)
    return _out_proj(y, w_out.astype(BF16), x)


def kernel(x, l0_norm, l0_w_in, l0_v_gain, l0_ws, l0_ws_bias, l0_w_out, l1_norm, l1_w_in, l1_b_f, l1_w_out, l2_norm, l2_w_in, l2_conv_w, l2_w_out, l3_norm, l3_w_in, l3_v_gain, l3_ws, l3_ws_bias, l3_w_out, final_norm):
    batch, seq, d_model = x.shape
    assert (batch, seq, d_model) == (1, SEQ, D_MODEL)
    xs = x.reshape(SEQ, D_MODEL)
    xs = _layer_a(xs, l0_norm, l0_w_in, l0_v_gain, l0_ws, l0_ws_bias, l0_w_out)
    xs = _layer_b(xs, l1_norm, l1_w_in, l1_b_f, l1_w_out)
    xs = _layer_c(xs, l2_norm, l2_w_in, l2_conv_w, l2_w_out)
    xs = _layer_a(xs, l3_norm, l3_w_in, l3_v_gain, l3_ws, l3_ws_bias, l3_w_out)
    return _rmsnorm(xs, final_norm, F32).reshape(batch, seq, d_model)
```

```python
import functools
import math

import jax
import jax.numpy as jnp
from jax import lax
from jax.experimental import pallas as pl
from jax.experimental.pallas import tpu as pltpu

F32 = jnp.float32
BF16 = jnp.bfloat16

SEQ = 8192
D_MODEL = 4096
D_INNER = 8192
HEAD_DIM = 128
GROUP_WIDTH = 128
CHUNK = 128
CONV_WIDTH = 3
RMS_EPS = 1e-6
NEG_INF = -1e30
LOG2E = math.log2(math.e)

LANES = 128
SUBLANES = 8
BF16_ROWS = 16
MIB = 1024 * 1024

PROJ_TM = 1024
PROJ_TN = 256
OUT_TM = 1024
OUT_TN = 1024
OUT_TK = 2048
ATT_TQ = 512
ATT_TK = 512
ATT_HEADS = 2
ATT_QSUB = 2
CUM_ROWS = 256
NORM_ROWS = 256
SPLIT = 3


def _params(semantics, vmem_mib):
    return pltpu.CompilerParams(dimension_semantics=semantics, vmem_limit_bytes=vmem_mib * MIB)


def _silu(x):
    return x * jax.nn.sigmoid(x)


def _dot(a, b):
    return jnp.dot(a, b, preferred_element_type=F32)


def _rmsnorm_kernel(x_ref, g_ref, o_ref):
    x = x_ref[...]
    r = lax.rsqrt(jnp.mean(x * x, axis=-1, keepdims=True) + RMS_EPS)
    o_ref[...] = (x * r * g_ref[...]).astype(o_ref.dtype)


def _rmsnorm(x, gain, out_dtype):
    rows, width = x.shape
    return pl.pallas_call(
        _rmsnorm_kernel,
        grid=(rows // NORM_ROWS,),
        in_specs=[pl.BlockSpec((NORM_ROWS, width), lambda i: (i, 0)),
                  pl.BlockSpec((1, width), lambda i: (0, 0))],
        out_specs=pl.BlockSpec((NORM_ROWS, width), lambda i: (i, 0)),
        out_shape=jax.ShapeDtypeStruct((rows, width), out_dtype),
        compiler_params=_params(("parallel",), 32),
        name="rmsnorm",
    )(x, gain.reshape(1, width))


def _n_col_tiles():
    return D_INNER // PROJ_TN


def _n_row_tiles():
    return SEQ // PROJ_TM


def _chunk_rows():
    return D_MODEL // _n_row_tiles()


def _proj_grid():
    return (_n_col_tiles() + 1, _n_row_tiles())


def _row_of(j, m):
    return jnp.where(j == 0, 0, m)


def _col_of(j):
    return jnp.maximum(j - 1, 0)


def _weight_specs(n_split):
    nb, nm = _n_col_tiles(), _n_row_tiles()

    def index(j, m, s):
        return (jnp.where(j == nb, nm - 1, m), s * nb + jnp.minimum(j, nb - 1))

    return [pl.BlockSpec((_chunk_rows(), PROJ_TN), functools.partial(index, s=s))
            for s in range(n_split)]


def _h_spec():
    return pl.BlockSpec((PROJ_TM, D_MODEL), lambda j, m: (_row_of(j, m), 0))


def _row_tile_spec():
    return pl.BlockSpec((PROJ_TM, PROJ_TN), lambda j, m: (_row_of(j, m), _col_of(j)))


def _col_tile_spec():
    return pl.BlockSpec((PROJ_TN, PROJ_TM), lambda j, m: (_col_of(j), _row_of(j, m)))


def _wslot_scratch(n_split):
    return pltpu.VMEM((2, D_MODEL, n_split * PROJ_TN), BF16)


def _stage_weights(w_refs, wslot_ref):
    j, m = pl.program_id(0), pl.program_id(1)
    rows = pl.ds(pl.multiple_of(m * _chunk_rows(), _chunk_rows()), _chunk_rows())
    for s, w_ref in enumerate(w_refs):
        wslot_ref[j % 2, rows, s * PROJ_TN:(s + 1) * PROJ_TN] = w_ref[...].astype(BF16)


def _projections(h_ref, wslot_ref, n_split):
    j = pl.program_id(0)
    h = h_ref[...]
    slot = (j + 1) % 2
    return [_dot(h, wslot_ref[slot, :, s * PROJ_TN:(s + 1) * PROJ_TN]) for s in range(n_split)]


def _proj_body(n_split, epilogue, first_step, h_ref, *refs):
    w_refs, rest = refs[:n_split], refs[n_split:]
    j = pl.program_id(0)

    @pl.when(j == 0)
    def _():
        _stage_weights(w_refs, rest[-1])
        if first_step is not None:
            first_step(*rest[:-1])

    @pl.when(j > 0)
    def _():
        _stage_weights(w_refs, rest[-1])
        epilogue(_projections(h_ref, rest[-1], n_split), *rest[:-1])


def _epilogue_a(z, ug_ref, v_ref):
    u, v, g = z
    ug_ref[...] = u * _silu(g)
    v_ref[...] = v


def _proj_a(h, w_in):
    return pl.pallas_call(
        functools.partial(_proj_body, 3, _epilogue_a, None),
        grid=_proj_grid(),
        in_specs=[_h_spec()] + _weight_specs(3),
        out_specs=[_row_tile_spec(), _row_tile_spec()],
        out_shape=[jax.ShapeDtypeStruct((SEQ, D_INNER), F32),
                   jax.ShapeDtypeStruct((SEQ, D_INNER), F32)],
        scratch_shapes=[_wslot_scratch(3)],
        compiler_params=_params(("arbitrary", "arbitrary"), 52),
        name="proj_a",
    )(h, w_in, w_in, w_in)


def _epilogue_b(z, qt_ref, k_ref, vt_ref, sg_ref):
    q, k, v, g = z
    qt_ref[...] = (q * (HEAD_DIM ** -0.5 * LOG2E)).T.astype(BF16)
    k_ref[...] = k.astype(BF16)
    vt_ref[...] = v.T.astype(BF16)
    sg_ref[...] = _silu(g)


def _proj_b(h, w_in):
    return pl.pallas_call(
        functools.partial(_proj_body, 4, _epilogue_b, None),
        grid=_proj_grid(),
        in_specs=[_h_spec()] + _weight_specs(4),
        out_specs=[_col_tile_spec(), _row_tile_spec(), _col_tile_spec(), _row_tile_spec()],
        out_shape=[jax.ShapeDtypeStruct((D_INNER, SEQ), BF16),
                   jax.ShapeDtypeStruct((SEQ, D_INNER), BF16),
                   jax.ShapeDtypeStruct((D_INNER, SEQ), BF16),
                   jax.ShapeDtypeStruct((SEQ, D_INNER), F32)],
        scratch_shapes=[_wslot_scratch(4)],
        compiler_params=_params(("arbitrary", "arbitrary"), 56),
        name="proj_b",
    )(h, w_in, w_in, w_in, w_in)


def _first_step_c(cw_ref, y_ref, carry_ref):
    carry_ref[...] = jnp.zeros_like(carry_ref)


def _epilogue_c(z, cw_ref, y_ref, carry_ref):
    bg, cg, hh, g = z
    m = pl.program_id(1)
    inner = cg * hh
    prev = jnp.where(m == 0, 0.0, carry_ref[...])
    p1 = prev[SUBLANES - 1:SUBLANES, :]
    p2 = prev[SUBLANES - 2:SUBLANES - 1, :]
    row = lax.broadcasted_iota(jnp.int32, inner.shape, 0)
    d1 = jnp.where(row == 0, p1, pltpu.roll(inner, 1, 0))
    d2 = jnp.where(row == 0, p2, jnp.where(row == 1, p1, pltpu.roll(inner, 2, 0)))
    cw = cw_ref[...]
    conv = cw[0:1, :] * d2 + cw[1:2, :] * d1 + cw[2:3, :] * inner
    carry_ref[...] = inner[PROJ_TM - SUBLANES:, :]
    y_ref[...] = (bg * conv * _silu(g)).astype(y_ref.dtype)


def _proj_c(h, w_in, conv_w):
    return pl.pallas_call(
        functools.partial(_proj_body, 4, _epilogue_c, _first_step_c),
        grid=_proj_grid(),
        in_specs=[_h_spec()] + _weight_specs(4)
        + [pl.BlockSpec((CONV_WIDTH, PROJ_TN), lambda j, m: (0, _col_of(j)))],
        out_specs=_row_tile_spec(),
        out_shape=jax.ShapeDtypeStruct((SEQ, D_INNER), BF16),
        scratch_shapes=[pltpu.VMEM((SUBLANES, PROJ_TN), F32), _wslot_scratch(4)],
        compiler_params=_params(("arbitrary", "arbitrary"), 56),
        name="proj_c",
    )(h, w_in, w_in, w_in, w_in, conv_w)


def _sgu_kernel(v_ref, ug_ref, gain_ref, ws_ref, bias_ref, y_ref, wsc_ref):
    n_groups = ws_ref.shape[0]

    @pl.when(pl.program_id(0) == 0)
    def _():
        t = lax.broadcasted_iota(jnp.int32, (CHUNK, CHUNK), 0)
        s = lax.broadcasted_iota(jnp.int32, (CHUNK, CHUNK), 1)
        for g in range(n_groups):
            wsc_ref[g] = jnp.where(t >= s, ws_ref[g], 0.0).astype(BF16)

    v = v_ref[...]
    r = lax.rsqrt(jnp.mean(v * v, axis=-1, keepdims=True) + RMS_EPS)
    for g in range(n_groups):
        cols = slice(g * GROUP_WIDTH, (g + 1) * GROUP_WIDTH)
        vn = (v_ref[:, cols] * r * gain_ref[:, cols]).astype(BF16)
        s = _dot(wsc_ref[g], vn) + bias_ref[g]
        y_ref[:, cols] = (ug_ref[:, cols] * s).astype(y_ref.dtype)


def _sgu(v, ug, v_gain, ws, ws_bias):
    n_groups = D_INNER // GROUP_WIDTH
    bias_b = jnp.broadcast_to(ws_bias[:, :, None], (n_groups, CHUNK, GROUP_WIDTH))
    return pl.pallas_call(
        _sgu_kernel,
        grid=(SEQ // CHUNK,),
        in_specs=[pl.BlockSpec((CHUNK, D_INNER), lambda n: (n, 0)),
                  pl.BlockSpec((CHUNK, D_INNER), lambda n: (n, 0)),
                  pl.BlockSpec((1, D_INNER), lambda n: (0, 0)),
                  pl.BlockSpec((n_groups, CHUNK, CHUNK), lambda n: (0, 0, 0)),
                  pl.BlockSpec((n_groups, CHUNK, GROUP_WIDTH), lambda n: (0, 0, 0))],
        out_specs=pl.BlockSpec((CHUNK, D_INNER), lambda n: (n, 0)),
        out_shape=jax.ShapeDtypeStruct((SEQ, D_INNER), BF16),
        scratch_shapes=[pltpu.VMEM((n_groups, CHUNK, CHUNK), BF16)],
        compiler_params=_params(("arbitrary",), 48),
        name="sgu",
    )(v, ug, v_gain.reshape(1, D_INNER), ws, bias_b)


def _split_bf16(x):
    pieces, rest = [], x
    for _ in range(SPLIT):
        piece = rest.astype(BF16).astype(F32)
        pieces.append(piece)
        rest = rest - piece
    return pieces


def _forget_cumsum_kernel(h_ref, wf_ref, bf_ref, kc_ref, qc_ref, carry_ref):
    n_heads = kc_ref.shape[0]

    @pl.when(pl.program_id(0) == 0)
    def _():
        carry_ref[...] = jnp.zeros_like(carry_ref)

    x = _dot(h_ref[...], wf_ref[...]) + bf_ref[...]
    log_f = jnp.minimum(x, 0.0) - jnp.log1p(jnp.exp(-jnp.abs(x)))
    t = lax.broadcasted_iota(jnp.int32, (CUM_ROWS, CUM_ROWS), 0)
    s = lax.broadcasted_iota(jnp.int32, (CUM_ROWS, CUM_ROWS), 1)
    tril = (t >= s).astype(F32)
    cum = jnp.dot(tril, log_f, preferred_element_type=F32,
                  precision=lax.Precision.HIGHEST) + carry_ref[...]
    carry_ref[...] = cum[CUM_ROWS - 1:CUM_ROWS, :]
    c2 = cum * LOG2E
    c2_t = c2.T
    lane = lax.broadcasted_iota(jnp.int32, (CUM_ROWS, LANES), 1)
    sub = lax.broadcasted_iota(jnp.int32, (BF16_ROWS, CUM_ROWS), 0)
    k_fill = jnp.where(lane < 2 * SPLIT, 1.0, 0.0)
    q_fill = jnp.where(sub < SPLIT, -1.0, 0.0)
    qc_ref[:, BF16_ROWS:, :] = jnp.zeros((n_heads, LANES - BF16_ROWS, CUM_ROWS), BF16)
    for hd in range(n_heads):
        k_tile = k_fill
        for i, piece in enumerate(_split_bf16(jnp.broadcast_to(c2[:, hd:hd + 1], (CUM_ROWS, LANES)))):
            k_tile = jnp.where(lane == i, piece, k_tile)
        kc_ref[hd] = k_tile.astype(BF16)
        q_tile = q_fill
        for i, piece in enumerate(_split_bf16(c2_t[hd:hd + 1, :])):
            q_tile = jnp.where(sub == SPLIT + i, piece, q_tile)
        qc_ref[hd, :BF16_ROWS, :] = q_tile.astype(BF16)


def _forget_cumsum(h, w_f, b_f):
    n_heads = D_INNER // HEAD_DIM
    return pl.pallas_call(
        _forget_cumsum_kernel,
        grid=(SEQ // CUM_ROWS,),
        in_specs=[pl.BlockSpec((CUM_ROWS, D_MODEL), lambda i: (i, 0)),
                  pl.BlockSpec((D_MODEL, LANES), lambda i: (0, 0)),
                  pl.BlockSpec((1, LANES), lambda i: (0, 0))],
        out_specs=[pl.BlockSpec((n_heads, CUM_ROWS, LANES), lambda i: (0, i, 0)),
                   pl.BlockSpec((n_heads, LANES, CUM_ROWS), lambda i: (0, 0, i))],
        out_shape=[jax.ShapeDtypeStruct((n_heads, SEQ, LANES), BF16),
                   jax.ShapeDtypeStruct((n_heads, LANES, SEQ), BF16)],
        scratch_shapes=[pltpu.VMEM((1, LANES), F32)],
        compiler_params=_params(("arbitrary",), 48),
        name="forget_cumsum",
    )(h, w_f, b_f)


def _col_reduce(op, x):
    group = 8 * SUBLANES
    parts = [x[r:r + group] for r in range(0, x.shape[0], group)]
    acc = parts[0]
    for part in parts[1:]:
        acc = op(acc, part)
    reduce = jnp.max if op is jnp.maximum else jnp.sum
    return reduce(acc, axis=0, keepdims=True)


def _attn_kernel(qt_ref, qc_ref, k_ref, kc_ref, vt_ref, sg_ref, y_ref, m_ref, l_ref, acc_ref, st0_ref):
    ip = pl.program_id(1)
    streams = [(hh, qs) for qs in range(ATT_QSUB) for hh in range(ATT_HEADS)]
    head = lambda hh: slice(hh * HEAD_DIM, (hh + 1) * HEAD_DIM)
    qcols = lambda qs: slice(qs * ATT_TQ, (qs + 1) * ATT_TQ)
    q_aug = {(hh, qs): jnp.concatenate([qt_ref[head(hh), qcols(qs)], qc_ref[hh, :, qcols(qs)]], axis=0)
             for hh, qs in streams}

    def scores(j, stream):
        hh, _ = stream
        keys = pl.ds(pl.multiple_of(j * ATT_TK, ATT_TK), ATT_TK)
        k_aug = jnp.concatenate([k_ref[keys, head(hh)], kc_ref[hh, keys, :]], axis=1)
        return _dot(k_aug, q_aug[stream])

    def absorb(j, stream, st, masked):
        hh, qs = stream
        n = streams.index(stream)
        keys = pl.ds(pl.multiple_of(j * ATT_TK, ATT_TK), ATT_TK)
        if masked:
            kpos = j * ATT_TK + lax.broadcasted_iota(jnp.int32, (ATT_TK, ATT_TQ), 0)
            qpos = (ip * ATT_QSUB + qs) * ATT_TQ + lax.broadcasted_iota(jnp.int32, (ATT_TK, ATT_TQ), 1)
            st = jnp.where(qpos >= kpos, st, NEG_INF)
        m = m_ref[n]
        m_new = jnp.maximum(m, _col_reduce(jnp.maximum, st))
        alpha = jnp.exp2(m - m_new)
        p = jnp.exp2(st - m_new)
        m_ref[n] = m_new
        l_ref[n] = alpha * l_ref[n] + _col_reduce(jnp.add, p)
        acc_ref[n] = alpha * acc_ref[n] + _dot(vt_ref[head(hh), keys], p.astype(BF16))

    def step(j, active, masked, rotated, prefetch):
        st = {active[0]: st0_ref[...] if rotated else scores(j, active[0])}
        for n, stream in enumerate(active):
            if n + 1 < len(active):
                st[active[n + 1]] = scores(j, active[n + 1])
            elif prefetch:
                st0_ref[...] = scores(j + 1, streams[0])
            absorb(j, stream, st.pop(stream), stream in masked)

    m_ref[...] = jnp.full(m_ref.shape, NEG_INF, F32)
    l_ref[...] = jnp.zeros(l_ref.shape, F32)
    acc_ref[...] = jnp.zeros(acc_ref.shape, F32)
    first_diag = ip * ATT_QSUB
    st0_ref[...] = scores(0, streams[0])

    @pl.loop(0, first_diag)
    def _(j):
        step(j, streams, (), True, True)

    for qd in range(ATT_QSUB):
        active = [stream for stream in streams if stream[1] >= qd]
        step(first_diag + qd, active, [stream for stream in active if stream[1] == qd], qd == 0, False)
    for n, (hh, qs) in enumerate(streams):
        o = (acc_ref[n] / l_ref[n]).T
        y_ref[qcols(qs), head(hh)] = (o * sg_ref[qcols(qs), head(hh)]).astype(y_ref.dtype)


def _attention(qt, qc, k, kc, vt, sg):
    width = ATT_HEADS * HEAD_DIM
    rows = ATT_QSUB * ATT_TQ
    return pl.pallas_call(
        _attn_kernel,
        grid=(D_INNER // width, SEQ // rows),
        in_specs=[pl.BlockSpec((width, rows), lambda h, i: (h, i)),
                  pl.BlockSpec((ATT_HEADS, LANES, rows), lambda h, i: (h, 0, i)),
                  pl.BlockSpec((SEQ, width), lambda h, i: (0, h)),
                  pl.BlockSpec((ATT_HEADS, SEQ, LANES), lambda h, i: (h, 0, 0)),
                  pl.BlockSpec((width, SEQ), lambda h, i: (h, 0)),
                  pl.BlockSpec((rows, width), lambda h, i: (i, h))],
        out_specs=pl.BlockSpec((rows, width), lambda h, i: (i, h)),
        out_shape=jax.ShapeDtypeStruct((SEQ, D_INNER), BF16),
        scratch_shapes=[pltpu.VMEM((ATT_HEADS * ATT_QSUB, 1, ATT_TQ), F32),
                        pltpu.VMEM((ATT_HEADS * ATT_QSUB, 1, ATT_TQ), F32),
                        pltpu.VMEM((ATT_HEADS * ATT_QSUB, HEAD_DIM, ATT_TQ), F32),
                        pltpu.VMEM((ATT_TK, ATT_TQ), F32)],
        compiler_params=_params(("parallel", "arbitrary"), 56),
        name="fox_attention",
    )(qt, qc, k, kc, vt, sg)


def _out_proj_kernel(y_ref, w_ref, x_ref, o_ref):
    kk = pl.program_id(2)
    part = _dot(y_ref[...], w_ref[...].astype(BF16))

    @pl.when(kk == 0)
    def _():
        o_ref[...] = x_ref[...] + part

    @pl.when(kk != 0)
    def _():
        o_ref[...] += part


def _out_proj(y, w_out, x):
    return pl.pallas_call(
        _out_proj_kernel,
        grid=(SEQ // OUT_TM, D_MODEL // OUT_TN, D_INNER // OUT_TK),
        in_specs=[pl.BlockSpec((OUT_TM, OUT_TK), lambda m, n, k: (m, k)),
                  pl.BlockSpec((OUT_TK, OUT_TN), lambda m, n, k: (k, n)),
                  pl.BlockSpec((OUT_TM, OUT_TN), lambda m, n, k: (m, n))],
        out_specs=pl.BlockSpec((OUT_TM, OUT_TN), lambda m, n, k: (m, n)),
        out_shape=jax.ShapeDtypeStruct((SEQ, D_MODEL), F32),
        compiler_params=_params(("parallel", "parallel", "arbitrary"), 56),
        name="out_proj",
    )(y, w_out, x)


def _layer_a(x, norm_g, w_in, v_gain, ws, ws_bias, w_out):
    h = _rmsnorm(x, norm_g, BF16)
    ug, v = _proj_a(h, w_in)
    y = _sgu(v, ug, v_gain, ws, ws_bias)
    return _out_proj(y, w_out, x)


def _layer_b(x, norm_g, w_in, b_f, w_out):
    h = _rmsnorm(x, norm_g, BF16)
    n_heads = D_INNER // HEAD_DIM
    pad = LANES - n_heads
    w_f = jnp.pad(w_in[:, 4 * D_INNER:], ((0, 0), (0, pad))).astype(BF16)
    b_pad = jnp.pad(b_f, (0, pad)).reshape(1, LANES)
    qt, k, vt, sg = _proj_b(h, w_in)
    kc, qc = _forget_cumsum(h, w_f, b_pad)
    y = _attention(qt, qc, k, kc, vt, sg)
    return _out_proj(y, w_out, x)


def _layer_c(x, norm_g, w_in, conv_w, w_out):
    h = _rmsnorm(x, norm_g, BF16)
    y = _proj_c(h, w_in, conv_w)
    return _out_proj(y, w_out, x)


def kernel(x, l0_norm, l0_w_in, l0_v_gain, l0_ws, l0_ws_bias, l0_w_out, l1_norm, l1_w_in, l1_b_f, l1_w_out, l2_norm, l2_w_in, l2_conv_w, l2_w_out, l3_norm, l3_w_in, l3_v_gain, l3_ws, l3_ws_bias, l3_w_out, final_norm):
    batch, seq, d_model = x.shape
    assert (batch, seq, d_model) == (1, SEQ, D_MODEL)
    xs = x.reshape(SEQ, D_MODEL)
    xs = _layer_a(xs, l0_norm, l0_w_in, l0_v_gain, l0_ws, l0_ws_bias, l0_w_out)
    xs = _layer_b(xs, l1_norm, l1_w_in, l1_b_f, l1_w_out)
    xs = _layer_c(xs, l2_norm, l2_w_in, l2_conv_w, l2_w_out)
    xs = _layer_a(xs, l3_norm, l3_w_in, l3_v_gain, l3_ws, l3_ws_bias, l3_w_out)
    return _rmsnorm(xs, final_norm, F32).reshape(batch, seq, d_model)
```

```python
import functools
import math
from typing import NamedTuple

import jax
import jax.numpy as jnp
from jax import lax
from jax.experimental import pallas as pl
from jax.experimental.pallas import tpu as pltpu

F32 = jnp.float32
BF16 = jnp.bfloat16

SEQ = 8192
D_MODEL = 4096
D_INNER = 8192
HEAD_DIM = 128
GROUP_WIDTH = 128
CHUNK = 128
CONV_WIDTH = 3
RMS_EPS = 1e-6
NEG_INF = -1e30
LOG2E = math.log2(math.e)

LANES = 128
SUBLANES = 8
BF16_ROWS = 16
MIB = 1024 * 1024

PROJ_TM = 1024
PROJ_TN = 256
OUT_TM = 512
OUT_TN = 512
ATT_TQ = 512
ATT_TK = 512
ATT_HEADS = 2
ATT_QSUB = 2
CUM_ROWS = 256
NORM_ROWS = 256
SPLIT = 3


def _params(semantics, vmem_mib):
    return pltpu.CompilerParams(dimension_semantics=semantics, vmem_limit_bytes=vmem_mib * MIB)


def _silu(x):
    return x * (0.5 * jnp.tanh(0.5 * x) + 0.5)


def _dot(a, b):
    return jnp.dot(a, b, preferred_element_type=F32)


def _dot_nt(a, b):
    return lax.dot_general(a, b, (((1,), (1,)), ((), ())), preferred_element_type=F32)


def _rmsnorm_kernel(x_ref, g_ref, o_ref):
    x = x_ref[...]
    r = lax.rsqrt(jnp.mean(x * x, axis=-1, keepdims=True) + RMS_EPS)
    o_ref[...] = (x * r * g_ref[...]).astype(o_ref.dtype)


def _rmsnorm(x, gain, out_dtype):
    rows, width = x.shape
    return pl.pallas_call(
        _rmsnorm_kernel,
        grid=(rows // NORM_ROWS,),
        in_specs=[pl.BlockSpec((NORM_ROWS, width), lambda i: (i, 0)),
                  pl.BlockSpec((1, width), lambda i: (0, 0))],
        out_specs=pl.BlockSpec((NORM_ROWS, width), lambda i: (i, 0)),
        out_shape=jax.ShapeDtypeStruct((rows, width), out_dtype),
        compiler_params=_params(("parallel",), 32),
        name="rmsnorm",
    )(x, gain.reshape(1, width))


class _Tiling(NamedTuple):
    k: int
    tm: int
    tn: int
    n_split: int
    split_stride: int
    transposed: bool

    @property
    def n_col_tiles(self):
        return self.split_stride // self.tn

    @property
    def n_row_tiles(self):
        return SEQ // self.tm

    @property
    def chunk_rows(self):
        return self.k // self.n_row_tiles

    @property
    def grid(self):
        return (self.n_col_tiles + 1, self.n_row_tiles)

    def row_of(self, j, m):
        return jnp.where(j == 0, 0, m)

    def col_of(self, j):
        return jnp.maximum(j - 1, 0)

    def act_spec(self):
        return pl.BlockSpec((self.tm, self.k), lambda j, m: (self.row_of(j, m), 0))

    def row_tile_spec(self):
        return pl.BlockSpec((self.tm, self.tn), lambda j, m: (self.row_of(j, m), self.col_of(j)))

    def col_tile_spec(self):
        return pl.BlockSpec((self.tn, self.tm), lambda j, m: (self.col_of(j), self.row_of(j, m)))

    def weight_specs(self):
        nb, nm = self.n_col_tiles, self.n_row_tiles

        def index(j, m, s):
            chunk, col = jnp.where(j == nb, nm - 1, m), s * nb + jnp.minimum(j, nb - 1)
            return (col, chunk) if self.transposed else (chunk, col)

        shape = (self.tn, self.chunk_rows) if self.transposed else (self.chunk_rows, self.tn)
        return [pl.BlockSpec(shape, functools.partial(index, s=s)) for s in range(self.n_split)]

    def slot_scratch(self):
        width = self.n_split * self.tn
        return pltpu.VMEM((2, width, self.k) if self.transposed else (2, self.k, width), BF16)


def _stage_weights(cfg, w_refs, wslot_ref):
    j, m = pl.program_id(0), pl.program_id(1)
    chunk = pl.ds(pl.multiple_of(m * cfg.chunk_rows, cfg.chunk_rows), cfg.chunk_rows)
    for s, w_ref in enumerate(w_refs):
        split = slice(s * cfg.tn, (s + 1) * cfg.tn)
        if cfg.transposed:
            wslot_ref[j % 2, split, chunk] = w_ref[...].astype(BF16)
        else:
            wslot_ref[j % 2, chunk, split] = w_ref[...].astype(BF16)


def _slot_product(cfg, a, wslot_ref, slot, s):
    split = slice(s * cfg.tn, (s + 1) * cfg.tn)
    if cfg.transposed:
        return _dot_nt(a, wslot_ref[slot, split, :])
    return _dot(a, wslot_ref[slot, :, split])


def _staged_body(cfg, epilogue, first_step, a_ref, *refs):
    w_refs, rest = refs[:cfg.n_split], refs[cfg.n_split:]
    wslot_ref = rest[-1]
    j = pl.program_id(0)

    @pl.when(j == 0)
    def _():
        _stage_weights(cfg, w_refs, wslot_ref)
        if first_step is not None:
            first_step(*rest[:-1])

    @pl.when(j > 0)
    def _():
        _stage_weights(cfg, w_refs, wslot_ref)
        a = a_ref[...]
        slot = (j + 1) % 2
        epilogue(lambda s: _slot_product(cfg, a, wslot_ref, slot, s), *rest[:-1])


def _staged_call(cfg, epilogue, first_step, a, w, extra_in, extra_specs, out_specs, out_shape,
                 scratch, vmem_mib, name):
    return pl.pallas_call(
        functools.partial(_staged_body, cfg, epilogue, first_step),
        grid=cfg.grid,
        in_specs=[cfg.act_spec()] + cfg.weight_specs() + extra_specs,
        out_specs=out_specs,
        out_shape=out_shape,
        scratch_shapes=scratch + [cfg.slot_scratch()],
        compiler_params=_params(("arbitrary", "arbitrary"), vmem_mib),
        name=name,
    )(a, *([w] * cfg.n_split), *extra_in)


def _proj_tiling(n_split, transposed=False):
    return _Tiling(D_MODEL, PROJ_TM, PROJ_TN, n_split, D_INNER, transposed)


def _epilogue_a(z, ug_ref, v_ref):
    sg = _silu(z(2))
    v_ref[...] = z(1).astype(v_ref.dtype)
    ug_ref[...] = (z(0) * sg).astype(ug_ref.dtype)


def _proj_a(h, w_in):
    cfg = _proj_tiling(3)
    return _staged_call(
        cfg, _epilogue_a, None, h, w_in, [], [],
        [cfg.row_tile_spec(), cfg.row_tile_spec()],
        [jax.ShapeDtypeStruct((SEQ, D_INNER), BF16), jax.ShapeDtypeStruct((SEQ, D_INNER), BF16)],
        [], 52, "proj_a")


def _epilogue_b(z, qt_ref, k_ref, vt_ref, sg_ref):
    qt_ref[...] = (z(0) * (HEAD_DIM ** -0.5 * LOG2E)).T.astype(BF16)
    vt_ref[...] = z(2).T.astype(BF16)
    sg_ref[...] = _silu(z(3))
    k_ref[...] = z(1).astype(BF16)


def _proj_b(h, w_in_t):
    cfg = _proj_tiling(4, transposed=True)
    return _staged_call(
        cfg, _epilogue_b, None, h, w_in_t, [], [],
        [cfg.col_tile_spec(), cfg.row_tile_spec(), cfg.col_tile_spec(), cfg.row_tile_spec()],
        [jax.ShapeDtypeStruct((D_INNER, SEQ), BF16), jax.ShapeDtypeStruct((SEQ, D_INNER), BF16),
         jax.ShapeDtypeStruct((D_INNER, SEQ), BF16), jax.ShapeDtypeStruct((SEQ, D_INNER), F32)],
        [], 56, "proj_b")


def _first_step_c(cw_ref, y_ref, carry_ref):
    carry_ref[...] = jnp.zeros_like(carry_ref)


def _epilogue_c(z, cw_ref, y_ref, carry_ref):
    m = pl.program_id(1)
    inner = z(1) * z(2)
    prev = jnp.where(m == 0, 0.0, carry_ref[...])
    p1 = prev[SUBLANES - 1:SUBLANES, :]
    p2 = prev[SUBLANES - 2:SUBLANES - 1, :]
    row = lax.broadcasted_iota(jnp.int32, inner.shape, 0)
    d1 = jnp.where(row == 0, p1, pltpu.roll(inner, 1, 0))
    d2 = jnp.where(row == 0, p2, jnp.where(row == 1, p1, pltpu.roll(inner, 2, 0)))
    cw = cw_ref[...]
    conv = cw[0:1, :] * d2 + cw[1:2, :] * d1 + cw[2:3, :] * inner
    carry_ref[...] = inner[PROJ_TM - SUBLANES:, :]
    gated = conv * _silu(z(3))
    y_ref[...] = (z(0) * gated).astype(y_ref.dtype)


def _proj_c(h, w_in, conv_w):
    cfg = _proj_tiling(4)
    return _staged_call(
        cfg, _epilogue_c, _first_step_c, h, w_in, [conv_w],
        [pl.BlockSpec((CONV_WIDTH, PROJ_TN), lambda j, m: (0, cfg.col_of(j)))],
        cfg.row_tile_spec(), jax.ShapeDtypeStruct((SEQ, D_INNER), BF16),
        [pltpu.VMEM((SUBLANES, PROJ_TN), F32)], 56, "proj_c")


def _epilogue_out(z, x_ref, o_ref):
    o_ref[...] = x_ref[...] + z(0)


def _out_proj(y, w_out, x):
    cfg = _Tiling(D_INNER, OUT_TM, OUT_TN, 1, D_MODEL, False)
    return _staged_call(
        cfg, _epilogue_out, None, y, w_out, [x], [cfg.row_tile_spec()],
        cfg.row_tile_spec(), jax.ShapeDtypeStruct((SEQ, D_MODEL), F32),
        [], 48, "out_proj")


def _sgu_kernel(v_ref, ug_ref, gain_ref, ws_ref, bias_ref, y_ref, wsc_ref):
    n_groups = ws_ref.shape[0]

    @pl.when(pl.program_id(0) == 0)
    def _():
        t = lax.broadcasted_iota(jnp.int32, (CHUNK, CHUNK), 0)
        s = lax.broadcasted_iota(jnp.int32, (CHUNK, CHUNK), 1)
        for g in range(n_groups):
            wsc_ref[g] = jnp.where(t >= s, ws_ref[g], 0.0).astype(BF16)

    v = v_ref[...].astype(F32)
    r = lax.rsqrt(jnp.mean(v * v, axis=-1, keepdims=True) + RMS_EPS)
    for g in range(n_groups):
        cols = slice(g * GROUP_WIDTH, (g + 1) * GROUP_WIDTH)
        vn = (v_ref[:, cols].astype(F32) * r * gain_ref[:, cols]).astype(BF16)
        s = _dot(wsc_ref[g], vn) + bias_ref[g]
        y_ref[:, cols] = (ug_ref[:, cols].astype(F32) * s).astype(y_ref.dtype)


def _sgu(v, ug, v_gain, ws, ws_bias):
    n_groups = D_INNER // GROUP_WIDTH
    bias_b = jnp.broadcast_to(ws_bias[:, :, None], (n_groups, CHUNK, GROUP_WIDTH))
    return pl.pallas_call(
        _sgu_kernel,
        grid=(SEQ // CHUNK,),
        in_specs=[pl.BlockSpec((CHUNK, D_INNER), lambda n: (n, 0)),
                  pl.BlockSpec((CHUNK, D_INNER), lambda n: (n, 0)),
                  pl.BlockSpec((1, D_INNER), lambda n: (0, 0)),
                  pl.BlockSpec((n_groups, CHUNK, CHUNK), lambda n: (0, 0, 0)),
                  pl.BlockSpec((n_groups, CHUNK, GROUP_WIDTH), lambda n: (0, 0, 0))],
        out_specs=pl.BlockSpec((CHUNK, D_INNER), lambda n: (n, 0)),
        out_shape=jax.ShapeDtypeStruct((SEQ, D_INNER), BF16),
        scratch_shapes=[pltpu.VMEM((n_groups, CHUNK, CHUNK), BF16)],
        compiler_params=_params(("arbitrary",), 48),
        name="sgu",
    )(v, ug, v_gain.reshape(1, D_INNER), ws, bias_b)


def _split_bf16(x):
    pieces, rest = [], x
    for _ in range(SPLIT):
        piece = rest.astype(BF16).astype(F32)
        pieces.append(piece)
        rest = rest - piece
    return pieces


def _forget_cumsum_kernel(h_ref, wf_ref, bf_ref, kc_ref, qc_ref, carry_ref, wpad_ref):
    n_heads = kc_ref.shape[0]

    @pl.when(pl.program_id(0) == 0)
    def _():
        carry_ref[...] = jnp.zeros_like(carry_ref)
        wpad_ref[:n_heads, :] = wf_ref[...].astype(BF16)
        wpad_ref[n_heads:, :] = jnp.zeros((LANES - n_heads, wpad_ref.shape[1]), BF16)

    x = _dot_nt(h_ref[...], wpad_ref[...]) + bf_ref[...]
    log_f = jnp.minimum(x, 0.0) - jnp.log1p(jnp.exp(-jnp.abs(x)))
    t = lax.broadcasted_iota(jnp.int32, (CUM_ROWS, CUM_ROWS), 0)
    s = lax.broadcasted_iota(jnp.int32, (CUM_ROWS, CUM_ROWS), 1)
    tril = (t >= s).astype(F32)
    cum = jnp.dot(tril, log_f, preferred_element_type=F32,
                  precision=lax.Precision.HIGHEST) + carry_ref[...]
    carry_ref[...] = cum[CUM_ROWS - 1:CUM_ROWS, :]
    c2 = cum * LOG2E
    c2_t = c2.T
    lane = lax.broadcasted_iota(jnp.int32, (CUM_ROWS, LANES), 1)
    sub = lax.broadcasted_iota(jnp.int32, (BF16_ROWS, CUM_ROWS), 0)
    k_fill = jnp.where(lane < 2 * SPLIT, 1.0, 0.0)
    q_fill = jnp.where(sub < SPLIT, -1.0, 0.0)
    qc_ref[:, BF16_ROWS:, :] = jnp.zeros((n_heads, LANES - BF16_ROWS, CUM_ROWS), BF16)
    for hd in range(n_heads):
        k_tile = k_fill
        for i, piece in enumerate(_split_bf16(jnp.broadcast_to(c2[:, hd:hd + 1], (CUM_ROWS, LANES)))):
            k_tile = jnp.where(lane == i, piece, k_tile)
        kc_ref[hd] = k_tile.astype(BF16)
        q_tile = q_fill
        for i, piece in enumerate(_split_bf16(c2_t[hd:hd + 1, :])):
            q_tile = jnp.where(sub == SPLIT + i, piece, q_tile)
        qc_ref[hd, :BF16_ROWS, :] = q_tile.astype(BF16)


def _forget_cumsum(h, w_in_t, b_f):
    n_heads = D_INNER // HEAD_DIM
    assert w_in_t.shape[0] == 4 * D_INNER + n_heads and (4 * D_INNER) % n_heads == 0
    return pl.pallas_call(
        _forget_cumsum_kernel,
        grid=(SEQ // CUM_ROWS,),
        in_specs=[pl.BlockSpec((CUM_ROWS, D_MODEL), lambda i: (i, 0)),
                  pl.BlockSpec((n_heads, D_MODEL), lambda i: (4 * D_INNER // n_heads, 0)),
                  pl.BlockSpec((1, LANES), lambda i: (0, 0))],
        out_specs=[pl.BlockSpec((n_heads, CUM_ROWS, LANES), lambda i: (0, i, 0)),
                   pl.BlockSpec((n_heads, LANES, CUM_ROWS), lambda i: (0, 0, i))],
        out_shape=[jax.ShapeDtypeStruct((n_heads, SEQ, LANES), BF16),
                   jax.ShapeDtypeStruct((n_heads, LANES, SEQ), BF16)],
        scratch_shapes=[pltpu.VMEM((1, LANES), F32), pltpu.VMEM((LANES, D_MODEL), BF16)],
        compiler_params=_params(("arbitrary",), 48),
        name="forget_cumsum",
    )(h, w_in_t, b_f)


def _col_reduce(op, x):
    group = 8 * SUBLANES
    parts = [x[r:r + group] for r in range(0, x.shape[0], group)]
    acc = parts[0]
    for part in parts[1:]:
        acc = op(acc, part)
    reduce = jnp.max if op is jnp.maximum else jnp.sum
    return reduce(acc, axis=0, keepdims=True)


def _attn_kernel(qt_ref, qc_ref, k_ref, kc_ref, vt_ref, sg_ref, y_ref, m_ref, acc_ref, st0_ref):
    ip = pl.program_id(1)
    streams = [(hh, qs) for qs in range(ATT_QSUB) for hh in range(ATT_HEADS)]
    head = lambda hh: slice(hh * HEAD_DIM, (hh + 1) * HEAD_DIM)
    qcols = lambda qs: slice(qs * ATT_TQ, (qs + 1) * ATT_TQ)
    q_aug = {(hh, qs): jnp.concatenate([qt_ref[head(hh), qcols(qs)], qc_ref[hh, :, qcols(qs)]], axis=0)
             for hh, qs in streams}

    def scores(j, stream):
        hh, _ = stream
        keys = pl.ds(pl.multiple_of(j * ATT_TK, ATT_TK), ATT_TK)
        k_aug = jnp.concatenate([k_ref[keys, head(hh)], kc_ref[hh, keys, :]], axis=1)
        return _dot(k_aug, q_aug[stream])

    def accumulate(j, stream, alpha, p):
        hh, _ = stream
        n = streams.index(stream)
        keys = pl.ds(pl.multiple_of(j * ATT_TK, ATT_TK), ATT_TK)
        v_aug = jnp.concatenate([vt_ref[head(hh), keys], jnp.ones((BF16_ROWS, ATT_TK), BF16)], axis=0)
        acc_ref[n] = alpha * acc_ref[n] + _dot(v_aug, p)

    def absorb(j, stream, st, masked):
        _, qs = stream
        n = streams.index(stream)
        if masked:
            kpos = j * ATT_TK + lax.broadcasted_iota(jnp.int32, (ATT_TK, ATT_TQ), 0)
            qpos = (ip * ATT_QSUB + qs) * ATT_TQ + lax.broadcasted_iota(jnp.int32, (ATT_TK, ATT_TQ), 1)
            st = jnp.where(qpos >= kpos, st, NEG_INF)
        m = m_ref[n]
        m_new = jnp.maximum(m, _col_reduce(jnp.maximum, st))
        alpha = jnp.exp2(m - m_new)
        p = jnp.exp2((st - m_new).astype(BF16))
        m_ref[n] = m_new
        accumulate(j, stream, alpha, p)

    def step(j, active, masked, rotated, prefetch):
        st = {active[0]: st0_ref[...] if rotated else scores(j, active[0])}
        for n, stream in enumerate(active):
            if n + 1 < len(active):
                st[active[n + 1]] = scores(j, active[n + 1])
            elif prefetch:
                st0_ref[...] = scores(j + 1, streams[0])
            absorb(j, stream, st.pop(stream), stream in masked)

    m_ref[...] = jnp.full(m_ref.shape, NEG_INF, F32)
    acc_ref[...] = jnp.zeros(acc_ref.shape, F32)
    first_diag = ip * ATT_QSUB
    st0_ref[...] = scores(0, streams[0])

    @pl.loop(0, first_diag)
    def _(j):
        step(j, streams, (), True, True)

    for qd in range(ATT_QSUB):
        active = [stream for stream in streams if stream[1] >= qd]
        step(first_diag + qd, active, [stream for stream in active if stream[1] == qd], qd == 0, False)
    for n, (hh, qs) in enumerate(streams):
        o = (acc_ref[n, :HEAD_DIM, :] / acc_ref[n, HEAD_DIM:HEAD_DIM + 1, :]).T
        y_ref[qcols(qs), head(hh)] = (o * sg_ref[qcols(qs), head(hh)]).astype(y_ref.dtype)


def _attention(qt, qc, k, kc, vt, sg):
    width = ATT_HEADS * HEAD_DIM
    rows = ATT_QSUB * ATT_TQ
    n_streams = ATT_HEADS * ATT_QSUB
    return pl.pallas_call(
        _attn_kernel,
        grid=(D_INNER // width, SEQ // rows),
        in_specs=[pl.BlockSpec((width, rows), lambda h, i: (h, i)),
                  pl.BlockSpec((ATT_HEADS, LANES, rows), lambda h, i: (h, 0, i)),
                  pl.BlockSpec((SEQ, width), lambda h, i: (0, h)),
                  pl.BlockSpec((ATT_HEADS, SEQ, LANES), lambda h, i: (h, 0, 0)),
                  pl.BlockSpec((width, SEQ), lambda h, i: (h, 0)),
                  pl.BlockSpec((rows, width), lambda h, i: (i, h))],
        out_specs=pl.BlockSpec((rows, width), lambda h, i: (i, h)),
        out_shape=jax.ShapeDtypeStruct((SEQ, D_INNER), BF16),
        scratch_shapes=[pltpu.VMEM((n_streams, 1, ATT_TQ), F32),
                        pltpu.VMEM((n_streams, HEAD_DIM + BF16_ROWS, ATT_TQ), F32),
                        pltpu.VMEM((ATT_TK, ATT_TQ), F32)],
        compiler_params=_params(("parallel", "arbitrary"), 56),
        name="fox_attention",
    )(qt, qc, k, kc, vt, sg)


def _layer_a(x, norm_g, w_in, v_gain, ws, ws_bias, w_out):
    h = _rmsnorm(x, norm_g, BF16)
    ug, v = _proj_a(h, w_in)
    y = _sgu(v, ug, v_gain, ws, ws_bias)
    return _out_proj(y, w_out, x)


def _layer_b(x, norm_g, w_in, b_f, w_out):
    h = _rmsnorm(x, norm_g, BF16)
    n_heads = D_INNER // HEAD_DIM
    pad = LANES - n_heads
    w_in_t = w_in.T
    b_pad = jnp.pad(b_f, (0, pad)).reshape(1, LANES)
    qt, k, vt, sg = _proj_b(h, w_in_t)
    kc, qc = _forget_cumsum(h, w_in_t, b_pad)
    y = _attention(qt, qc, k, kc, vt, sg)
    return _out_proj(y, w_out, x)


def _layer_c(x, norm_g, w_in, conv_w, w_out):
    h = _rmsnorm(x, norm_g, BF16)
    y = _proj_c(h, w_in, conv_w)
    return _out_proj(y, w_out, x)


def kernel(x, l0_norm, l0_w_in, l0_v_gain, l0_ws, l0_ws_bias, l0_w_out, l1_norm, l1_w_in, l1_b_f, l1_w_out, l2_norm, l2_w_in, l2_conv_w, l2_w_out, l3_norm, l3_w_in, l3_v_gain, l3_ws, l3_ws_bias, l3_w_out, final_norm):
    batch, seq, d_model = x.shape
    assert (batch, seq, d_model) == (1, SEQ, D_MODEL)
    xs = x.reshape(SEQ, D_MODEL)
    xs = _layer_a(xs, l0_norm, l0_w_in, l0_v_gain, l0_ws, l0_ws_bias, l0_w_out)
    xs = _layer_b(xs, l1_norm, l1_w_in, l1_b_f, l1_w_out)
    xs = _layer_c(xs, l2_norm, l2_w_in, l2_conv_w, l2_w_out)
    xs = _layer_a(xs, l3_norm, l3_w_in, l3_v_gain, l3_ws, l3_ws_bias, l3_w_out)
    return _rmsnorm(xs, final_norm, F32).reshape(batch, seq, d_model)
```

```python
import functools
import math
from typing import NamedTuple

import jax
import jax.numpy as jnp
from jax import lax
from jax.experimental import pallas as pl
from jax.experimental.pallas import tpu as pltpu

F32 = jnp.float32
BF16 = jnp.bfloat16

SEQ = 8192
D_MODEL = 4096
D_INNER = 8192
HEAD_DIM = 128
GROUP_WIDTH = 128
CHUNK = 128
CONV_WIDTH = 3
RMS_EPS = 1e-6
NEG_INF = -1e30
LOG2E = math.log2(math.e)

LANES = 128
SUBLANES = 8
BF16_ROWS = 16
MIB = 1024 * 1024

PROJ_TM = 1024
PROJ_TN = 256
OUT_TM = 512
OUT_TN = 512
ATT_TQ = 512
ATT_TK = 512
ATT_HEADS = 2
ATT_QSUB = 2
CUM_ROWS = 256
NORM_ROWS = 256
SPLIT = 3


def _params(semantics, vmem_mib):
    return pltpu.CompilerParams(dimension_semantics=semantics, vmem_limit_bytes=vmem_mib * MIB)


def _silu(x):
    return x * (0.5 * jnp.tanh(0.5 * x) + 0.5)


def _dot(a, b):
    return jnp.dot(a, b, preferred_element_type=F32)


def _dot_nt(a, b):
    return lax.dot_general(a, b, (((1,), (1,)), ((), ())), preferred_element_type=F32)


def _rmsnorm_kernel(x_ref, g_ref, o_ref):
    x = x_ref[...]
    r = lax.rsqrt(jnp.mean(x * x, axis=-1, keepdims=True) + RMS_EPS)
    o_ref[...] = (x * r * g_ref[...]).astype(o_ref.dtype)


def _rmsnorm(x, gain, out_dtype):
    rows, width = x.shape
    return pl.pallas_call(
        _rmsnorm_kernel,
        grid=(rows // NORM_ROWS,),
        in_specs=[pl.BlockSpec((NORM_ROWS, width), lambda i: (i, 0)),
                  pl.BlockSpec((1, width), lambda i: (0, 0))],
        out_specs=pl.BlockSpec((NORM_ROWS, width), lambda i: (i, 0)),
        out_shape=jax.ShapeDtypeStruct((rows, width), out_dtype),
        compiler_params=_params(("parallel",), 32),
        name="rmsnorm",
    )(x, gain.reshape(1, width))


class _Tiling(NamedTuple):
    k: int
    tm: int
    tn: int
    n_split: int
    split_stride: int
    transposed: bool

    @property
    def n_col_tiles(self):
        return self.split_stride // self.tn

    @property
    def n_row_tiles(self):
        return SEQ // self.tm

    @property
    def chunk_rows(self):
        return self.k // self.n_row_tiles

    @property
    def grid(self):
        return (self.n_col_tiles + 1, self.n_row_tiles)

    def row_of(self, j, m):
        return jnp.where(j == 0, 0, m)

    def col_of(self, j):
        return jnp.maximum(j - 1, 0)

    def act_spec(self):
        return pl.BlockSpec((self.tm, self.k), lambda j, m: (self.row_of(j, m), 0))

    def row_tile_spec(self):
        return pl.BlockSpec((self.tm, self.tn), lambda j, m: (self.row_of(j, m), self.col_of(j)))

    def col_tile_spec(self):
        return pl.BlockSpec((self.tn, self.tm), lambda j, m: (self.col_of(j), self.row_of(j, m)))

    def weight_specs(self):
        nb, nm = self.n_col_tiles, self.n_row_tiles

        def index(j, m, s):
            chunk, col = jnp.where(j == nb, nm - 1, m), s * nb + jnp.minimum(j, nb - 1)
            return (col, chunk) if self.transposed else (chunk, col)

        shape = (self.tn, self.chunk_rows) if self.transposed else (self.chunk_rows, self.tn)
        return [pl.BlockSpec(shape, functools.partial(index, s=s)) for s in range(self.n_split)]

    def slot_scratch(self):
        width = self.n_split * self.tn
        return pltpu.VMEM((2, width, self.k) if self.transposed else (2, self.k, width), BF16)


def _stage_weights(cfg, w_refs, wslot_ref):
    j, m = pl.program_id(0), pl.program_id(1)
    chunk = pl.ds(pl.multiple_of(m * cfg.chunk_rows, cfg.chunk_rows), cfg.chunk_rows)
    for s, w_ref in enumerate(w_refs):
        split = slice(s * cfg.tn, (s + 1) * cfg.tn)
        if cfg.transposed:
            wslot_ref[j % 2, split, chunk] = w_ref[...].astype(BF16)
        else:
            wslot_ref[j % 2, chunk, split] = w_ref[...].astype(BF16)


def _slot_product(cfg, a, wslot_ref, slot, s):
    split = slice(s * cfg.tn, (s + 1) * cfg.tn)
    if cfg.transposed:
        return _dot_nt(a, wslot_ref[slot, split, :])
    return _dot(a, wslot_ref[slot, :, split])


def _staged_body(cfg, epilogue, first_step, a_ref, *refs):
    w_refs, rest = refs[:cfg.n_split], refs[cfg.n_split:]
    wslot_ref = rest[-1]
    j = pl.program_id(0)

    @pl.when(j == 0)
    def _():
        _stage_weights(cfg, w_refs, wslot_ref)
        if first_step is not None:
            first_step(*rest[:-1])

    @pl.when(j > 0)
    def _():
        _stage_weights(cfg, w_refs, wslot_ref)
        a = a_ref[...]
        slot = (j + 1) % 2
        epilogue(lambda s: _slot_product(cfg, a, wslot_ref, slot, s), *rest[:-1])


def _staged_call(cfg, epilogue, first_step, a, w, extra_in, extra_specs, out_specs, out_shape,
                 scratch, vmem_mib, name):
    return pl.pallas_call(
        functools.partial(_staged_body, cfg, epilogue, first_step),
        grid=cfg.grid,
        in_specs=[cfg.act_spec()] + cfg.weight_specs() + extra_specs,
        out_specs=out_specs,
        out_shape=out_shape,
        scratch_shapes=scratch + [cfg.slot_scratch()],
        compiler_params=_params(("arbitrary", "arbitrary"), vmem_mib),
        name=name,
    )(a, *([w] * cfg.n_split), *extra_in)


def _proj_tiling(n_split, transposed=False):
    return _Tiling(D_MODEL, PROJ_TM, PROJ_TN, n_split, D_INNER, transposed)


def _epilogue_a(z, ug_ref, v_ref):
    sg = _silu(z(2))
    v_ref[...] = z(1).astype(v_ref.dtype)
    ug_ref[...] = (z(0) * sg).astype(ug_ref.dtype)


def _proj_a(h, w_in):
    cfg = _proj_tiling(3)
    return _staged_call(
        cfg, _epilogue_a, None, h, w_in, [], [],
        [cfg.row_tile_spec(), cfg.row_tile_spec()],
        [jax.ShapeDtypeStruct((SEQ, D_INNER), BF16), jax.ShapeDtypeStruct((SEQ, D_INNER), BF16)],
        [], 52, "proj_a")


def _epilogue_b(z, qt_ref, k_ref, vt_ref, sg_ref):
    qt_ref[...] = (z(0) * (HEAD_DIM ** -0.5 * LOG2E)).T.astype(BF16)
    vt_ref[...] = z(2).T.astype(BF16)
    sg_ref[...] = _silu(z(3))
    k_ref[...] = z(1).astype(BF16)


def _proj_b(h, w_in_t):
    cfg = _proj_tiling(4, transposed=True)
    return _staged_call(
        cfg, _epilogue_b, None, h, w_in_t, [], [],
        [cfg.col_tile_spec(), cfg.row_tile_spec(), cfg.col_tile_spec(), cfg.row_tile_spec()],
        [jax.ShapeDtypeStruct((D_INNER, SEQ), BF16), jax.ShapeDtypeStruct((SEQ, D_INNER), BF16),
         jax.ShapeDtypeStruct((D_INNER, SEQ), BF16), jax.ShapeDtypeStruct((SEQ, D_INNER), F32)],
        [], 56, "proj_b")


def _first_step_c(cw_ref, y_ref, carry_ref):
    carry_ref[...] = jnp.zeros_like(carry_ref)


def _epilogue_c(z, cw_ref, y_ref, carry_ref):
    m = pl.program_id(1)
    inner = z(1) * z(2)
    prev = jnp.where(m == 0, 0.0, carry_ref[...])
    p1 = prev[SUBLANES - 1:SUBLANES, :]
    p2 = prev[SUBLANES - 2:SUBLANES - 1, :]
    row = lax.broadcasted_iota(jnp.int32, inner.shape, 0)
    d1 = jnp.where(row == 0, p1, pltpu.roll(inner, 1, 0))
    d2 = jnp.where(row == 0, p2, jnp.where(row == 1, p1, pltpu.roll(inner, 2, 0)))
    cw = cw_ref[...]
    conv = cw[0:1, :] * d2 + cw[1:2, :] * d1 + cw[2:3, :] * inner
    carry_ref[...] = inner[PROJ_TM - SUBLANES:, :]
    gated = conv * _silu(z(3))
    y_ref[...] = (z(0) * gated).astype(y_ref.dtype)


def _proj_c(h, w_in, conv_w):
    cfg = _proj_tiling(4)
    return _staged_call(
        cfg, _epilogue_c, _first_step_c, h, w_in, [conv_w],
        [pl.BlockSpec((CONV_WIDTH, PROJ_TN), lambda j, m: (0, cfg.col_of(j)))],
        cfg.row_tile_spec(), jax.ShapeDtypeStruct((SEQ, D_INNER), BF16),
        [pltpu.VMEM((SUBLANES, PROJ_TN), F32)], 56, "proj_c")


def _epilogue_out(z, x_ref, o_ref):
    o_ref[...] = x_ref[...] + z(0)


def _out_proj(y, w_out, x):
    cfg = _Tiling(D_INNER, OUT_TM, OUT_TN, 1, D_MODEL, False)
    return _staged_call(
        cfg, _epilogue_out, None, y, w_out, [x], [cfg.row_tile_spec()],
        cfg.row_tile_spec(), jax.ShapeDtypeStruct((SEQ, D_MODEL), F32),
        [], 48, "out_proj")


def _sgu_kernel(v_ref, ug_ref, gain_ref, ws_ref, bias_ref, y_ref, wsc_ref):
    n_groups = ws_ref.shape[0]

    @pl.when(pl.program_id(0) == 0)
    def _():
        t = lax.broadcasted_iota(jnp.int32, (CHUNK, CHUNK), 0)
        s = lax.broadcasted_iota(jnp.int32, (CHUNK, CHUNK), 1)
        for g in range(n_groups):
            wsc_ref[g] = jnp.where(t >= s, ws_ref[g], 0.0).astype(BF16)

    v = v_ref[...].astype(F32)
    r = lax.rsqrt(jnp.mean(v * v, axis=-1, keepdims=True) + RMS_EPS)
    for g in range(n_groups):
        cols = slice(g * GROUP_WIDTH, (g + 1) * GROUP_WIDTH)
        vn = (v_ref[:, cols].astype(F32) * r * gain_ref[:, cols]).astype(BF16)
        s = _dot(wsc_ref[g], vn) + bias_ref[g]
        y_ref[:, cols] = (ug_ref[:, cols].astype(F32) * s).astype(y_ref.dtype)


def _sgu(v, ug, v_gain, ws, ws_bias):
    n_groups = D_INNER // GROUP_WIDTH
    bias_b = jnp.broadcast_to(ws_bias[:, :, None], (n_groups, CHUNK, GROUP_WIDTH))
    return pl.pallas_call(
        _sgu_kernel,
        grid=(SEQ // CHUNK,),
        in_specs=[pl.BlockSpec((CHUNK, D_INNER), lambda n: (n, 0)),
                  pl.BlockSpec((CHUNK, D_INNER), lambda n: (n, 0)),
                  pl.BlockSpec((1, D_INNER), lambda n: (0, 0)),
                  pl.BlockSpec((n_groups, CHUNK, CHUNK), lambda n: (0, 0, 0)),
                  pl.BlockSpec((n_groups, CHUNK, GROUP_WIDTH), lambda n: (0, 0, 0))],
        out_specs=pl.BlockSpec((CHUNK, D_INNER), lambda n: (n, 0)),
        out_shape=jax.ShapeDtypeStruct((SEQ, D_INNER), BF16),
        scratch_shapes=[pltpu.VMEM((n_groups, CHUNK, CHUNK), BF16)],
        compiler_params=_params(("arbitrary",), 48),
        name="sgu",
    )(v, ug, v_gain.reshape(1, D_INNER), ws, bias_b)


DECAY_GROUP = 32
assert SPLIT * DECAY_GROUP <= LANES - SPLIT


def _split_bf16(x):
    pieces, rest = [], x
    for _ in range(SPLIT):
        piece = rest.astype(BF16).astype(F32)
        pieces.append(piece)
        rest = rest - piece
    return pieces


def _forget_cumsum_kernel(h_ref, wf_ref, bf_ref, kc_ref, qc_ref, carry_ref, wpad_ref):
    n_heads = qc_ref.shape[0]

    @pl.when(pl.program_id(0) == 0)
    def _():
        carry_ref[...] = jnp.zeros_like(carry_ref)
        wpad_ref[:n_heads, :] = wf_ref[...].astype(BF16)
        wpad_ref[n_heads:, :] = jnp.zeros((LANES - n_heads, wpad_ref.shape[1]), BF16)

    x = _dot_nt(h_ref[...], wpad_ref[...]) + bf_ref[...]
    log_f = jnp.minimum(x, 0.0) - jnp.log1p(jnp.exp(-jnp.abs(x)))
    t = lax.broadcasted_iota(jnp.int32, (CUM_ROWS, CUM_ROWS), 0)
    s = lax.broadcasted_iota(jnp.int32, (CUM_ROWS, CUM_ROWS), 1)
    tril = (t >= s).astype(F32)
    cum = jnp.dot(tril, log_f, preferred_element_type=F32,
                  precision=lax.Precision.HIGHEST) + carry_ref[...]
    carry_ref[...] = cum[CUM_ROWS - 1:CUM_ROWS, :]
    c2 = cum * LOG2E
    pieces = _split_bf16(c2)
    head_row = lax.broadcasted_iota(jnp.int32, (LANES, LANES), 0)
    lane_col = lax.broadcasted_iota(jnp.int32, (LANES, LANES), 1)
    lane = lax.broadcasted_iota(jnp.int32, (CUM_ROWS, LANES), 1)
    for g in range(kc_ref.shape[0]):
        local = head_row - g * DECAY_GROUP
        in_group = (local >= 0) & (local < DECAY_GROUP)
        tile = jnp.zeros((CUM_ROWS, LANES), F32)
        for i, piece in enumerate(pieces):
            select = jnp.where(in_group & (lane_col == SPLIT * local + i), 1.0, 0.0).astype(BF16)
            tile = tile + _dot(piece.astype(BF16), select)
        kc_ref[g] = jnp.where(lane >= LANES - SPLIT, 1.0, tile).astype(BF16)
    sub = lax.broadcasted_iota(jnp.int32, (BF16_ROWS, CUM_ROWS), 0)
    pieces_t = [piece.T for piece in pieces]
    for hd in range(n_heads):
        q_tile = jnp.zeros((BF16_ROWS, CUM_ROWS), F32)
        for i, piece_t in enumerate(pieces_t):
            q_tile = jnp.where(sub == BF16_ROWS - SPLIT + i, piece_t[hd:hd + 1, :], q_tile)
        qc_ref[hd] = q_tile.astype(BF16)


def _forget_cumsum(h, w_in_t, b_f):
    n_heads = D_INNER // HEAD_DIM
    n_groups = pl.cdiv(n_heads, DECAY_GROUP)
    assert w_in_t.shape[0] == 4 * D_INNER + n_heads and (4 * D_INNER) % n_heads == 0
    return pl.pallas_call(
        _forget_cumsum_kernel,
        grid=(SEQ // CUM_ROWS,),
        in_specs=[pl.BlockSpec((CUM_ROWS, D_MODEL), lambda i: (i, 0)),
                  pl.BlockSpec((n_heads, D_MODEL), lambda i: (4 * D_INNER // n_heads, 0)),
                  pl.BlockSpec((1, LANES), lambda i: (0, 0))],
        out_specs=[pl.BlockSpec((n_groups, CUM_ROWS, LANES), lambda i: (0, i, 0)),
                   pl.BlockSpec((n_heads, BF16_ROWS, CUM_ROWS), lambda i: (0, 0, i))],
        out_shape=[jax.ShapeDtypeStruct((n_groups, SEQ, LANES), BF16),
                   jax.ShapeDtypeStruct((n_heads, BF16_ROWS, SEQ), BF16)],
        scratch_shapes=[pltpu.VMEM((1, LANES), F32), pltpu.VMEM((LANES, D_MODEL), BF16)],
        compiler_params=_params(("arbitrary",), 48),
        name="forget_cumsum",
    )(h, w_in_t, b_f)


def _col_reduce(op, x):
    group = 8 * SUBLANES
    parts = [x[r:r + group] for r in range(0, x.shape[0], group)]
    acc = parts[0]
    for part in parts[1:]:
        acc = op(acc, part)
    reduce = jnp.max if op is jnp.maximum else jnp.sum
    return reduce(acc, axis=0, keepdims=True)


def _attn_kernel(qt_ref, qc_ref, k_ref, kc_ref, vt_ref, sg_ref, y_ref, m_ref, acc_ref, st0_ref):
    ip = pl.program_id(1)
    streams = [(hh, qs) for qs in range(ATT_QSUB) for hh in range(ATT_HEADS)]
    head = lambda hh: slice(hh * HEAD_DIM, (hh + 1) * HEAD_DIM)
    qcols = lambda qs: slice(qs * ATT_TQ, (qs + 1) * ATT_TQ)
    sub = lax.broadcasted_iota(jnp.int32, (LANES - BF16_ROWS, ATT_TQ), 0)
    q_aug = {}
    for hh in range(ATT_HEADS):
        first = SPLIT * ((pl.program_id(0) * ATT_HEADS + hh) % DECAY_GROUP)
        minus = jnp.where((sub >= first) & (sub < first + SPLIT), -1.0, 0.0).astype(BF16)
        for qs in range(ATT_QSUB):
            q_aug[(hh, qs)] = jnp.concatenate(
                [qt_ref[head(hh), qcols(qs)], minus, qc_ref[hh, :, qcols(qs)]], axis=0)

    def scores(j, stream):
        hh, _ = stream
        keys = pl.ds(pl.multiple_of(j * ATT_TK, ATT_TK), ATT_TK)
        k_aug = jnp.concatenate([k_ref[keys, head(hh)], kc_ref[keys, :]], axis=1)
        return _dot(k_aug, q_aug[stream])

    def accumulate(j, stream, alpha, p):
        hh, _ = stream
        n = streams.index(stream)
        keys = pl.ds(pl.multiple_of(j * ATT_TK, ATT_TK), ATT_TK)
        v_aug = jnp.concatenate([vt_ref[head(hh), keys], jnp.ones((BF16_ROWS, ATT_TK), BF16)], axis=0)
        acc_ref[n] = alpha * acc_ref[n] + _dot(v_aug, p)

    def absorb(j, stream, st, masked):
        _, qs = stream
        n = streams.index(stream)
        if masked:
            kpos = j * ATT_TK + lax.broadcasted_iota(jnp.int32, (ATT_TK, ATT_TQ), 0)
            qpos = (ip * ATT_QSUB + qs) * ATT_TQ + lax.broadcasted_iota(jnp.int32, (ATT_TK, ATT_TQ), 1)
            st = jnp.where(qpos >= kpos, st, NEG_INF)
        m = m_ref[n]
        m_new = jnp.maximum(m, _col_reduce(jnp.maximum, st))
        alpha = jnp.exp2(m - m_new)
        p = jnp.exp2((st - m_new).astype(BF16))
        m_ref[n] = m_new
        accumulate(j, stream, alpha, p)

    def step(j, active, masked, rotated, prefetch):
        st = {active[0]: st0_ref[...] if rotated else scores(j, active[0])}
        for n, stream in enumerate(active):
            if n + 1 < len(active):
                st[active[n + 1]] = scores(j, active[n + 1])
            elif prefetch:
                st0_ref[...] = scores(j + 1, streams[0])
            absorb(j, stream, st.pop(stream), stream in masked)

    m_ref[...] = jnp.full(m_ref.shape, NEG_INF, F32)
    acc_ref[...] = jnp.zeros(acc_ref.shape, F32)
    st0_ref[...] = scores(0, streams[0])

    @pl.loop(0, ip)
    def _(jj):
        for u in range(ATT_QSUB):
            step(jj * ATT_QSUB + u, streams, (), True, True)

    for qd in range(ATT_QSUB):
        active = [stream for stream in streams if stream[1] >= qd]
        step(ip * ATT_QSUB + qd, active, [stream for stream in active if stream[1] == qd], qd == 0, False)
    for n, (hh, qs) in enumerate(streams):
        o = (acc_ref[n, :HEAD_DIM, :] / acc_ref[n, HEAD_DIM:HEAD_DIM + 1, :]).T
        y_ref[qcols(qs), head(hh)] = (o * sg_ref[qcols(qs), head(hh)]).astype(y_ref.dtype)


def _attention(qt, qc, k, kc, vt, sg):
    width = ATT_HEADS * HEAD_DIM
    rows = ATT_QSUB * ATT_TQ
    n_streams = ATT_HEADS * ATT_QSUB
    assert DECAY_GROUP % ATT_HEADS == 0
    return pl.pallas_call(
        _attn_kernel,
        grid=(D_INNER // width, SEQ // rows),
        in_specs=[pl.BlockSpec((width, rows), lambda h, i: (h, i)),
                  pl.BlockSpec((ATT_HEADS, BF16_ROWS, rows), lambda h, i: (h, 0, i)),
                  pl.BlockSpec((SEQ, width), lambda h, i: (0, h)),
                  pl.BlockSpec((None, SEQ, LANES), lambda h, i: (h * ATT_HEADS // DECAY_GROUP, 0, 0)),
                  pl.BlockSpec((width, SEQ), lambda h, i: (h, 0)),
                  pl.BlockSpec((rows, width), lambda h, i: (i, h))],
        out_specs=pl.BlockSpec((rows, width), lambda h, i: (i, h)),
        out_shape=jax.ShapeDtypeStruct((SEQ, D_INNER), BF16),
        scratch_shapes=[pltpu.VMEM((n_streams, 1, ATT_TQ), F32),
                        pltpu.VMEM((n_streams, HEAD_DIM + BF16_ROWS, ATT_TQ), F32),
                        pltpu.VMEM((ATT_TK, ATT_TQ), F32)],
        compiler_params=_params(("parallel", "arbitrary"), 56),
        name="fox_attention",
    )(qt, qc, k, kc, vt, sg)


def _layer_a(x, norm_g, w_in, v_gain, ws, ws_bias, w_out):
    h = _rmsnorm(x, norm_g, BF16)
    ug, v = _proj_a(h, w_in)
    y = _sgu(v, ug, v_gain, ws, ws_bias)
    return _out_proj(y, w_out, x)


def _layer_b(x, norm_g, w_in, b_f, w_out):
    h = _rmsnorm(x, norm_g, BF16)
    n_heads = D_INNER // HEAD_DIM
    pad = LANES - n_heads
    w_in_t = w_in.T
    b_pad = jnp.pad(b_f, (0, pad)).reshape(1, LANES)
    qt, k, vt, sg = _proj_b(h, w_in_t)
    kc, qc = _forget_cumsum(h, w_in_t, b_pad)
    y = _attention(qt, qc, k, kc, vt, sg)
    return _out_proj(y, w_out, x)


def _layer_c(x, norm_g, w_in, conv_w, w_out):
    h = _rmsnorm(x, norm_g, BF16)
    y = _proj_c(h, w_in, conv_w)
    return _out_proj(y, w_out, x)


def kernel(x, l0_norm, l0_w_in, l0_v_gain, l0_ws, l0_ws_bias, l0_w_out, l1_norm, l1_w_in, l1_b_f, l1_w_out, l2_norm, l2_w_in, l2_conv_w, l2_w_out, l3_norm, l3_w_in, l3_v_gain, l3_ws, l3_ws_bias, l3_w_out, final_norm):
    batch, seq, d_model = x.shape
    assert (batch, seq, d_model) == (1, SEQ, D_MODEL)
    xs = x.reshape(SEQ, D_MODEL)
    xs = _layer_a(xs, l0_norm, l0_w_in, l0_v_gain, l0_ws, l0_ws_bias, l0_w_out)
    xs = _layer_b(xs, l1_norm, l1_w_in, l1_b_f, l1_w_out)
    xs = _layer_c(xs, l2_norm, l2_w_in, l2_conv_w, l2_w_out)
    xs = _layer_a(xs, l3_norm, l3_w_in, l3_v_gain, l3_ws, l3_ws_bias, l3_w_out)
    return _rmsnorm(xs, final_norm, F32).reshape(batch, seq, d_model)
```

```python
import functools
import math
from typing import NamedTuple

import jax
import jax.numpy as jnp
from jax import lax
from jax.experimental import pallas as pl
from jax.experimental.pallas import tpu as pltpu

F32 = jnp.float32
BF16 = jnp.bfloat16

SEQ = 8192
D_MODEL = 4096
D_INNER = 8192
HEAD_DIM = 128
GROUP_WIDTH = 128
CHUNK = 128
CONV_WIDTH = 3
RMS_EPS = 1e-6
NEG_INF = -1e30
LOG2E = math.log2(math.e)

LANES = 128
SUBLANES = 8
BF16_ROWS = 16
MIB = 1024 * 1024

PROJ_TM = 1024
PROJ_TN = 256
OUT_TM = 512
OUT_TN = 512
ATT_TQ = 512
ATT_TK = 512
ATT_HEADS = 2
ATT_QSUB = 2
CUM_ROWS = 256
NORM_ROWS = 512
SGU_CHUNKS = 2
SPLIT = 3


def _params(semantics, vmem_mib):
    return pltpu.CompilerParams(dimension_semantics=semantics, vmem_limit_bytes=vmem_mib * MIB)


def _silu(x):
    return x * (0.5 * jnp.tanh(0.5 * x) + 0.5)


def _dot(a, b):
    return jnp.dot(a, b, preferred_element_type=F32)


def _dot_nt(a, b):
    return lax.dot_general(a, b, (((1,), (1,)), ((), ())), preferred_element_type=F32)


def _rmsnorm_kernel(x_ref, g_ref, o_ref):
    x = x_ref[...]
    r = lax.rsqrt(jnp.mean(x * x, axis=-1, keepdims=True) + RMS_EPS)
    o_ref[...] = (x * r * g_ref[...]).astype(o_ref.dtype)


def _rmsnorm(x, gain, out_dtype):
    rows, width = x.shape
    return pl.pallas_call(
        _rmsnorm_kernel,
        grid=(rows // NORM_ROWS,),
        in_specs=[pl.BlockSpec((NORM_ROWS, width), lambda i: (i, 0)),
                  pl.BlockSpec((1, width), lambda i: (0, 0))],
        out_specs=pl.BlockSpec((NORM_ROWS, width), lambda i: (i, 0)),
        out_shape=jax.ShapeDtypeStruct((rows, width), out_dtype),
        compiler_params=_params(("parallel",), 40),
        name="rmsnorm",
    )(x, gain.reshape(1, width))


class _Tiling(NamedTuple):
    k: int
    tm: int
    tn: int
    n_split: int
    split_stride: int
    transposed: bool

    @property
    def n_col_tiles(self):
        return self.split_stride // self.tn

    @property
    def n_row_tiles(self):
        return SEQ // self.tm

    @property
    def chunk_rows(self):
        return self.k // self.n_row_tiles

    @property
    def grid(self):
        return (self.n_col_tiles + 1, self.n_row_tiles)

    def row_of(self, j, m):
        return jnp.where(j == 0, 0, m)

    def col_of(self, j):
        return jnp.maximum(j - 1, 0)

    def act_spec(self):
        return pl.BlockSpec((self.tm, self.k), lambda j, m: (self.row_of(j, m), 0))

    def row_tile_spec(self):
        return pl.BlockSpec((self.tm, self.tn), lambda j, m: (self.row_of(j, m), self.col_of(j)))

    def col_tile_spec(self):
        return pl.BlockSpec((self.tn, self.tm), lambda j, m: (self.col_of(j), self.row_of(j, m)))

    def weight_specs(self):
        nb, nm = self.n_col_tiles, self.n_row_tiles

        def index(j, m, s):
            chunk, col = jnp.where(j == nb, nm - 1, m), s * nb + jnp.minimum(j, nb - 1)
            return (col, chunk) if self.transposed else (chunk, col)

        shape = (self.tn, self.chunk_rows) if self.transposed else (self.chunk_rows, self.tn)
        return [pl.BlockSpec(shape, functools.partial(index, s=s)) for s in range(self.n_split)]

    def slot_scratch(self):
        width = self.n_split * self.tn
        return pltpu.VMEM((2, width, self.k) if self.transposed else (2, self.k, width), BF16)


def _stage_weights(cfg, w_refs, wslot_ref):
    j, m = pl.program_id(0), pl.program_id(1)
    chunk = pl.ds(pl.multiple_of(m * cfg.chunk_rows, cfg.chunk_rows), cfg.chunk_rows)
    for s, w_ref in enumerate(w_refs):
        split = slice(s * cfg.tn, (s + 1) * cfg.tn)
        if cfg.transposed:
            wslot_ref[j % 2, split, chunk] = w_ref[...].astype(BF16)
        else:
            wslot_ref[j % 2, chunk, split] = w_ref[...].astype(BF16)


def _slot_product(cfg, a, wslot_ref, slot, s):
    split = slice(s * cfg.tn, (s + 1) * cfg.tn)
    if cfg.transposed:
        return _dot_nt(a, wslot_ref[slot, split, :])
    return _dot(a, wslot_ref[slot, :, split])


def _staged_body(cfg, epilogue, first_step, a_ref, *refs):
    w_refs, rest = refs[:cfg.n_split], refs[cfg.n_split:]
    wslot_ref = rest[-1]
    j = pl.program_id(0)

    @pl.when(j == 0)
    def _():
        _stage_weights(cfg, w_refs, wslot_ref)
        if first_step is not None:
            first_step(*rest[:-1])

    @pl.when(j > 0)
    def _():
        _stage_weights(cfg, w_refs, wslot_ref)
        a = a_ref[...]
        slot = (j + 1) % 2
        epilogue(lambda s: _slot_product(cfg, a, wslot_ref, slot, s), *rest[:-1])


def _staged_call(cfg, epilogue, first_step, a, w, extra_in, extra_specs, out_specs, out_shape,
                 scratch, vmem_mib, name):
    return pl.pallas_call(
        functools.partial(_staged_body, cfg, epilogue, first_step),
        grid=cfg.grid,
        in_specs=[cfg.act_spec()] + cfg.weight_specs() + extra_specs,
        out_specs=out_specs,
        out_shape=out_shape,
        scratch_shapes=scratch + [cfg.slot_scratch()],
        compiler_params=_params(("arbitrary", "arbitrary"), vmem_mib),
        name=name,
    )(a, *([w] * cfg.n_split), *extra_in)


def _proj_tiling(n_split, transposed=False):
    return _Tiling(D_MODEL, PROJ_TM, PROJ_TN, n_split, D_INNER, transposed)


def _epilogue_a(z, ug_ref, v_ref):
    sg = _silu(z(2))
    v_ref[...] = z(1).astype(v_ref.dtype)
    ug_ref[...] = (z(0) * sg).astype(ug_ref.dtype)


def _proj_a(h, w_in):
    cfg = _proj_tiling(3)
    return _staged_call(
        cfg, _epilogue_a, None, h, w_in, [], [],
        [cfg.row_tile_spec(), cfg.row_tile_spec()],
        [jax.ShapeDtypeStruct((SEQ, D_INNER), BF16), jax.ShapeDtypeStruct((SEQ, D_INNER), BF16)],
        [], 52, "proj_a")


def _epilogue_b(z, qt_ref, k_ref, vt_ref, sg_ref):
    qt_ref[...] = (z(0) * (HEAD_DIM ** -0.5 * LOG2E)).T.astype(BF16)
    vt_ref[...] = z(2).T.astype(BF16)
    sg_ref[...] = _silu(z(3))
    k_ref[...] = z(1).astype(BF16)


def _proj_b(h, w_in_t):
    cfg = _proj_tiling(4, transposed=True)
    return _staged_call(
        cfg, _epilogue_b, None, h, w_in_t, [], [],
        [cfg.col_tile_spec(), cfg.row_tile_spec(), cfg.col_tile_spec(), cfg.row_tile_spec()],
        [jax.ShapeDtypeStruct((D_INNER, SEQ), BF16), jax.ShapeDtypeStruct((SEQ, D_INNER), BF16),
         jax.ShapeDtypeStruct((D_INNER, SEQ), BF16), jax.ShapeDtypeStruct((SEQ, D_INNER), F32)],
        [], 56, "proj_b")


def _first_step_c(cw_ref, y_ref, carry_ref):
    carry_ref[...] = jnp.zeros_like(carry_ref)


def _epilogue_c(z, cw_ref, y_ref, carry_ref):
    m = pl.program_id(1)
    inner = z(1) * z(2)
    prev = jnp.where(m == 0, 0.0, carry_ref[...])
    p1 = prev[SUBLANES - 1:SUBLANES, :]
    p2 = prev[SUBLANES - 2:SUBLANES - 1, :]
    row = lax.broadcasted_iota(jnp.int32, inner.shape, 0)
    d1 = jnp.where(row == 0, p1, pltpu.roll(inner, 1, 0))
    d2 = jnp.where(row == 0, p2, jnp.where(row == 1, p1, pltpu.roll(inner, 2, 0)))
    cw = cw_ref[...]
    conv = cw[0:1, :] * d2 + cw[1:2, :] * d1 + cw[2:3, :] * inner
    carry_ref[...] = inner[PROJ_TM - SUBLANES:, :]
    gated = conv * _silu(z(3))
    y_ref[...] = (z(0) * gated).astype(y_ref.dtype)


def _proj_c(h, w_in, conv_w):
    cfg = _proj_tiling(4)
    return _staged_call(
        cfg, _epilogue_c, _first_step_c, h, w_in, [conv_w],
        [pl.BlockSpec((CONV_WIDTH, PROJ_TN), lambda j, m: (0, cfg.col_of(j)))],
        cfg.row_tile_spec(), jax.ShapeDtypeStruct((SEQ, D_INNER), BF16),
        [pltpu.VMEM((SUBLANES, PROJ_TN), F32)], 56, "proj_c")


def _epilogue_out(z, x_ref, o_ref):
    o_ref[...] = x_ref[...] + z(0)


def _out_proj(y, w_out, x):
    cfg = _Tiling(D_INNER, OUT_TM, OUT_TN, 1, D_MODEL, False)
    return _staged_call(
        cfg, _epilogue_out, None, y, w_out, [x], [cfg.row_tile_spec()],
        cfg.row_tile_spec(), jax.ShapeDtypeStruct((SEQ, D_MODEL), F32),
        [], 48, "out_proj")


def _sgu_kernel(v_ref, ug_ref, gain_ref, ws_ref, bias_ref, y_ref, wsc_ref):
    n_groups = ws_ref.shape[0]

    @pl.when(pl.program_id(0) == 0)
    def _():
        t = lax.broadcasted_iota(jnp.int32, (CHUNK, CHUNK), 0)
        s = lax.broadcasted_iota(jnp.int32, (CHUNK, CHUNK), 1)
        for g in range(n_groups):
            wsc_ref[g] = jnp.where(t >= s, ws_ref[g], 0.0).astype(BF16)

    groups = [slice(g * GROUP_WIDTH, (g + 1) * GROUP_WIDTH) for g in range(n_groups)]
    chunks = [slice(c * CHUNK, (c + 1) * CHUNK) for c in range(SGU_CHUNKS)]
    squares = jnp.zeros((SGU_CHUNKS * CHUNK, GROUP_WIDTH), F32)
    for cols in groups:
        x = v_ref[:, cols].astype(F32)
        squares = squares + x * x
    r = lax.rsqrt(jnp.sum(squares, axis=-1, keepdims=True) * (1.0 / v_ref.shape[1]) + RMS_EPS)
    for g, cols in enumerate(groups):
        vn = (v_ref[:, cols].astype(F32) * r * gain_ref[:, cols]).astype(BF16)
        s = _dot(wsc_ref[g], jnp.concatenate([vn[rows] for rows in chunks], axis=1))
        for c, rows in enumerate(chunks):
            s_c = s[:, c * GROUP_WIDTH:(c + 1) * GROUP_WIDTH] + bias_ref[g]
            y_ref[rows, cols] = (ug_ref[rows, cols].astype(F32) * s_c).astype(y_ref.dtype)


def _sgu(v, ug, v_gain, ws, ws_bias):
    n_groups = D_INNER // GROUP_WIDTH
    rows = SGU_CHUNKS * CHUNK
    bias_b = jnp.broadcast_to(ws_bias[:, :, None], (n_groups, CHUNK, GROUP_WIDTH))
    return pl.pallas_call(
        _sgu_kernel,
        grid=(SEQ // rows,),
        in_specs=[pl.BlockSpec((rows, D_INNER), lambda n: (n, 0)),
                  pl.BlockSpec((rows, D_INNER), lambda n: (n, 0)),
                  pl.BlockSpec((1, D_INNER), lambda n: (0, 0)),
                  pl.BlockSpec((n_groups, CHUNK, CHUNK), lambda n: (0, 0, 0)),
                  pl.BlockSpec((n_groups, CHUNK, GROUP_WIDTH), lambda n: (0, 0, 0))],
        out_specs=pl.BlockSpec((rows, D_INNER), lambda n: (n, 0)),
        out_shape=jax.ShapeDtypeStruct((SEQ, D_INNER), BF16),
        scratch_shapes=[pltpu.VMEM((n_groups, CHUNK, CHUNK), BF16)],
        compiler_params=_params(("arbitrary",), 48),
        name="sgu",
    )(v, ug, v_gain.reshape(1, D_INNER), ws, bias_b)


DECAY_GROUP = 32
assert SPLIT * DECAY_GROUP <= LANES - SPLIT


def _split_bf16(x):
    pieces, rest = [], x
    for _ in range(SPLIT):
        piece = rest.astype(BF16).astype(F32)
        pieces.append(piece)
        rest = rest - piece
    return pieces


def _forget_cumsum_kernel(h_ref, wf_ref, bf_ref, kc_ref, qc_ref, carry_ref, wpad_ref):
    n_heads = qc_ref.shape[0]

    @pl.when(pl.program_id(0) == 0)
    def _():
        carry_ref[...] = jnp.zeros_like(carry_ref)
        wpad_ref[:n_heads, :] = wf_ref[...].astype(BF16)
        wpad_ref[n_heads:, :] = jnp.zeros((LANES - n_heads, wpad_ref.shape[1]), BF16)

    x = _dot_nt(h_ref[...], wpad_ref[...]) + bf_ref[...]
    log_f = jnp.minimum(x, 0.0) - jnp.log1p(jnp.exp(-jnp.abs(x)))
    t = lax.broadcasted_iota(jnp.int32, (CUM_ROWS, CUM_ROWS), 0)
    s = lax.broadcasted_iota(jnp.int32, (CUM_ROWS, CUM_ROWS), 1)
    tril = (t >= s).astype(F32)
    cum = jnp.dot(tril, log_f, preferred_element_type=F32,
                  precision=lax.Precision.HIGHEST) + carry_ref[...]
    carry_ref[...] = cum[CUM_ROWS - 1:CUM_ROWS, :]
    c2 = cum * LOG2E
    pieces = _split_bf16(c2)
    head_row = lax.broadcasted_iota(jnp.int32, (LANES, LANES), 0)
    lane_col = lax.broadcasted_iota(jnp.int32, (LANES, LANES), 1)
    lane = lax.broadcasted_iota(jnp.int32, (CUM_ROWS, LANES), 1)
    for g in range(kc_ref.shape[0]):
        local = head_row - g * DECAY_GROUP
        in_group = (local >= 0) & (local < DECAY_GROUP)
        tile = jnp.zeros((CUM_ROWS, LANES), F32)
        for i, piece in enumerate(pieces):
            select = jnp.where(in_group & (lane_col == SPLIT * local + i), 1.0, 0.0).astype(BF16)
            tile = tile + _dot(piece.astype(BF16), select)
        kc_ref[g] = jnp.where(lane >= LANES - SPLIT, 1.0, tile).astype(BF16)
    sub = lax.broadcasted_iota(jnp.int32, (BF16_ROWS, CUM_ROWS), 0)
    pieces_t = [piece.T for piece in pieces]
    for hd in range(n_heads):
        q_tile = jnp.zeros((BF16_ROWS, CUM_ROWS), F32)
        for i, piece_t in enumerate(pieces_t):
            q_tile = jnp.where(sub == BF16_ROWS - SPLIT + i, piece_t[hd:hd + 1, :], q_tile)
        qc_ref[hd] = q_tile.astype(BF16)


def _forget_cumsum(h, w_in_t, b_f):
    n_heads = D_INNER // HEAD_DIM
    n_groups = pl.cdiv(n_heads, DECAY_GROUP)
    assert w_in_t.shape[0] == 4 * D_INNER + n_heads and (4 * D_INNER) % n_heads == 0
    return pl.pallas_call(
        _forget_cumsum_kernel,
        grid=(SEQ // CUM_ROWS,),
        in_specs=[pl.BlockSpec((CUM_ROWS, D_MODEL), lambda i: (i, 0)),
                  pl.BlockSpec((n_heads, D_MODEL), lambda i: (4 * D_INNER // n_heads, 0)),
                  pl.BlockSpec((1, LANES), lambda i: (0, 0))],
        out_specs=[pl.BlockSpec((n_groups, CUM_ROWS, LANES), lambda i: (0, i, 0)),
                   pl.BlockSpec((n_heads, BF16_ROWS, CUM_ROWS), lambda i: (0, 0, i))],
        out_shape=[jax.ShapeDtypeStruct((n_groups, SEQ, LANES), BF16),
                   jax.ShapeDtypeStruct((n_heads, BF16_ROWS, SEQ), BF16)],
        scratch_shapes=[pltpu.VMEM((1, LANES), F32), pltpu.VMEM((LANES, D_MODEL), BF16)],
        compiler_params=_params(("arbitrary",), 48),
        name="forget_cumsum",
    )(h, w_in_t, b_f)


def _col_reduce(op, x):
    group = 8 * SUBLANES
    parts = [x[r:r + group] for r in range(0, x.shape[0], group)]
    acc = parts[0]
    for part in parts[1:]:
        acc = op(acc, part)
    reduce = jnp.max if op is jnp.maximum else jnp.sum
    return reduce(acc, axis=0, keepdims=True)


def _attn_kernel(qt_ref, qc_ref, k_ref, kc_ref, vt_ref, sg_ref, y_ref, m_ref, acc_ref, st0_ref):
    ip = pl.program_id(1)
    streams = [(hh, qs) for qs in range(ATT_QSUB) for hh in range(ATT_HEADS)]
    head = lambda hh: slice(hh * HEAD_DIM, (hh + 1) * HEAD_DIM)
    qcols = lambda qs: slice(qs * ATT_TQ, (qs + 1) * ATT_TQ)
    sub = lax.broadcasted_iota(jnp.int32, (LANES - BF16_ROWS, ATT_TQ), 0)
    q_aug = {}
    for hh in range(ATT_HEADS):
        first = SPLIT * ((pl.program_id(0) * ATT_HEADS + hh) % DECAY_GROUP)
        minus = jnp.where((sub >= first) & (sub < first + SPLIT), -1.0, 0.0).astype(BF16)
        for qs in range(ATT_QSUB):
            q_aug[(hh, qs)] = jnp.concatenate(
                [qt_ref[head(hh), qcols(qs)], minus, qc_ref[hh, :, qcols(qs)]], axis=0)

    def scores(j, stream):
        hh, _ = stream
        keys = pl.ds(pl.multiple_of(j * ATT_TK, ATT_TK), ATT_TK)
        k_aug = jnp.concatenate([k_ref[keys, head(hh)], kc_ref[keys, :]], axis=1)
        return _dot(k_aug, q_aug[stream])

    def accumulate(j, stream, alpha, p):
        hh, _ = stream
        n = streams.index(stream)
        keys = pl.ds(pl.multiple_of(j * ATT_TK, ATT_TK), ATT_TK)
        v_aug = jnp.concatenate([vt_ref[head(hh), keys], jnp.ones((BF16_ROWS, ATT_TK), BF16)], axis=0)
        acc_ref[n] = alpha * acc_ref[n] + _dot(v_aug, p)

    def absorb(j, stream, st, masked):
        _, qs = stream
        n = streams.index(stream)
        if masked:
            kpos = j * ATT_TK + lax.broadcasted_iota(jnp.int32, (ATT_TK, ATT_TQ), 0)
            qpos = (ip * ATT_QSUB + qs) * ATT_TQ + lax.broadcasted_iota(jnp.int32, (ATT_TK, ATT_TQ), 1)
            st = jnp.where(qpos >= kpos, st, NEG_INF)
        m = m_ref[n]
        m_new = jnp.maximum(m, _col_reduce(jnp.maximum, st))
        alpha = jnp.exp2(m - m_new)
        p = jnp.exp2((st - m_new).astype(BF16))
        m_ref[n] = m_new
        accumulate(j, stream, alpha, p)

    def step(j, active, masked, rotated, prefetch):
        st = {active[0]: st0_ref[...] if rotated else scores(j, active[0])}
        for n, stream in enumerate(active):
            if n + 1 < len(active):
                st[active[n + 1]] = scores(j, active[n + 1])
            elif prefetch:
                st0_ref[...] = scores(j + 1, streams[0])
            absorb(j, stream, st.pop(stream), stream in masked)

    m_ref[...] = jnp.full(m_ref.shape, NEG_INF, F32)
    acc_ref[...] = jnp.zeros(acc_ref.shape, F32)
    st0_ref[...] = scores(0, streams[0])

    @pl.loop(0, ip)
    def _(jj):
        for u in range(ATT_QSUB):
            step(jj * ATT_QSUB + u, streams, (), True, True)

    for qd in range(ATT_QSUB):
        active = [stream for stream in streams if stream[1] >= qd]
        step(ip * ATT_QSUB + qd, active, [stream for stream in active if stream[1] == qd], qd == 0, False)
    for n, (hh, qs) in enumerate(streams):
        o = (acc_ref[n, :HEAD_DIM, :] / acc_ref[n, HEAD_DIM:HEAD_DIM + 1, :]).T
        y_ref[qcols(qs), head(hh)] = (o * sg_ref[qcols(qs), head(hh)]).astype(y_ref.dtype)


def _attention(qt, qc, k, kc, vt, sg):
    width = ATT_HEADS * HEAD_DIM
    rows = ATT_QSUB * ATT_TQ
    n_streams = ATT_HEADS * ATT_QSUB
    assert DECAY_GROUP % ATT_HEADS == 0
    return pl.pallas_call(
        _attn_kernel,
        grid=(D_INNER // width, SEQ // rows),
        in_specs=[pl.BlockSpec((width, rows), lambda h, i: (h, i)),
                  pl.BlockSpec((ATT_HEADS, BF16_ROWS, rows), lambda h, i: (h, 0, i)),
                  pl.BlockSpec((SEQ, width), lambda h, i: (0, h)),
                  pl.BlockSpec((None, SEQ, LANES), lambda h, i: (h * ATT_HEADS // DECAY_GROUP, 0, 0)),
                  pl.BlockSpec((width, SEQ), lambda h, i: (h, 0)),
                  pl.BlockSpec((rows, width), lambda h, i: (i, h))],
        out_specs=pl.BlockSpec((rows, width), lambda h, i: (i, h)),
        out_shape=jax.ShapeDtypeStruct((SEQ, D_INNER), BF16),
        scratch_shapes=[pltpu.VMEM((n_streams, 1, ATT_TQ), F32),
                        pltpu.VMEM((n_streams, HEAD_DIM + BF16_ROWS, ATT_TQ), F32),
                        pltpu.VMEM((ATT_TK, ATT_TQ), F32)],
        compiler_params=_params(("parallel", "arbitrary"), 56),
        name="fox_attention",
    )(qt, qc, k, kc, vt, sg)


def _layer_a(x, norm_g, w_in, v_gain, ws, ws_bias, w_out):
    h = _rmsnorm(x, norm_g, BF16)
    ug, v = _proj_a(h, w_in)
    y = _sgu(v, ug, v_gain, ws, ws_bias)
    return _out_proj(y, w_out, x)


def _layer_b(x, norm_g, w_in, b_f, w_out):
    h = _rmsnorm(x, norm_g, BF16)
    n_heads = D_INNER // HEAD_DIM
    pad = LANES - n_heads
    w_in_t = w_in.T
    b_pad = jnp.pad(b_f, (0, pad)).reshape(1, LANES)
    qt, k, vt, sg = _proj_b(h, w_in_t)
    kc, qc = _forget_cumsum(h, w_in_t, b_pad)
    y = _attention(qt, qc, k, kc, vt, sg)
    return _out_proj(y, w_out, x)


def _layer_c(x, norm_g, w_in, conv_w, w_out):
    h = _rmsnorm(x, norm_g, BF16)
    y = _proj_c(h, w_in, conv_w)
    return _out_proj(y, w_out, x)


def kernel(x, l0_norm, l0_w_in, l0_v_gain, l0_ws, l0_ws_bias, l0_w_out, l1_norm, l1_w_in, l1_b_f, l1_w_out, l2_norm, l2_w_in, l2_conv_w, l2_w_out, l3_norm, l3_w_in, l3_v_gain, l3_ws, l3_ws_bias, l3_w_out, final_norm):
    batch, seq, d_model = x.shape
    assert (batch, seq, d_model) == (1, SEQ, D_MODEL)
    xs = x.reshape(SEQ, D_MODEL)
    xs = _layer_a(xs, l0_norm, l0_w_in, l0_v_gain, l0_ws, l0_ws_bias, l0_w_out)
    xs = _layer_b(xs, l1_norm, l1_w_in, l1_b_f, l1_w_out)
    xs = _layer_c(xs, l2_norm, l2_w_in, l2_conv_w, l2_w_out)
    xs = _layer_a(xs, l3_norm, l3_w_in, l3_v_gain, l3_ws, l3_ws_bias, l3_w_out)
    return _rmsnorm(xs, final_norm, F32).reshape(batch, seq, d_model)
```

```python
import functools
import math
from typing import NamedTuple

import jax
import jax.numpy as jnp
from jax import lax
from jax.experimental import pallas as pl
from jax.experimental.pallas import tpu as pltpu

F32 = jnp.float32
BF16 = jnp.bfloat16

SEQ = 8192
D_MODEL = 4096
D_INNER = 8192
HEAD_DIM = 128
GROUP_WIDTH = 128
CHUNK = 128
CONV_WIDTH = 3
RMS_EPS = 1e-6
NEG_INF = -1e30
LOG2E = math.log2(math.e)

LANES = 128
SUBLANES = 8
BF16_ROWS = 16
MIB = 1024 * 1024

PROJ_TM = 1024
PROJ_TN = 256
OUT_TM = 512
OUT_TN = 512
ATT_TQ = 512
ATT_TK = 512
ATT_HEADS = 2
ATT_QSUB = 4
CUM_ROWS = 256
NORM_ROWS = 512
SGU_CHUNKS = 2
SPLIT = 3


def _params(semantics, vmem_mib):
    return pltpu.CompilerParams(dimension_semantics=semantics, vmem_limit_bytes=vmem_mib * MIB)


def _silu(x):
    return x * (0.5 * jnp.tanh(0.5 * x) + 0.5)


def _dot(a, b):
    return jnp.dot(a, b, preferred_element_type=F32)


def _dot_nt(a, b):
    return lax.dot_general(a, b, (((1,), (1,)), ((), ())), preferred_element_type=F32)


def _rmsnorm_kernel(x_ref, g_ref, o_ref):
    x = x_ref[...]
    r = lax.rsqrt(jnp.mean(x * x, axis=-1, keepdims=True) + RMS_EPS)
    o_ref[...] = (x * r * g_ref[...]).astype(o_ref.dtype)


def _rmsnorm(x, gain, out_dtype):
    rows, width = x.shape
    return pl.pallas_call(
        _rmsnorm_kernel,
        grid=(rows // NORM_ROWS,),
        in_specs=[pl.BlockSpec((NORM_ROWS, width), lambda i: (i, 0)),
                  pl.BlockSpec((1, width), lambda i: (0, 0))],
        out_specs=pl.BlockSpec((NORM_ROWS, width), lambda i: (i, 0)),
        out_shape=jax.ShapeDtypeStruct((rows, width), out_dtype),
        compiler_params=_params(("parallel",), 40),
        name="rmsnorm",
    )(x, gain.reshape(1, width))


class _Tiling(NamedTuple):
    k: int
    tm: int
    tn: int
    n_split: int
    split_stride: int
    transposed: bool

    @property
    def n_col_tiles(self):
        return self.split_stride // self.tn

    @property
    def n_row_tiles(self):
        return SEQ // self.tm

    @property
    def chunk_rows(self):
        return self.k // self.n_row_tiles

    @property
    def grid(self):
        return (self.n_col_tiles + 1, self.n_row_tiles)

    def row_of(self, j, m):
        return jnp.where(j == 0, 0, m)

    def col_of(self, j):
        return jnp.maximum(j - 1, 0)

    def act_spec(self):
        return pl.BlockSpec((self.tm, self.k), lambda j, m: (self.row_of(j, m), 0))

    def row_tile_spec(self):
        return pl.BlockSpec((self.tm, self.tn), lambda j, m: (self.row_of(j, m), self.col_of(j)))

    def col_tile_spec(self):
        return pl.BlockSpec((self.tn, self.tm), lambda j, m: (self.col_of(j), self.row_of(j, m)))

    def weight_specs(self):
        nb, nm = self.n_col_tiles, self.n_row_tiles

        def index(j, m, s):
            chunk, col = jnp.where(j == nb, nm - 1, m), s * nb + jnp.minimum(j, nb - 1)
            return (col, chunk) if self.transposed else (chunk, col)

        shape = (self.tn, self.chunk_rows) if self.transposed else (self.chunk_rows, self.tn)
        return [pl.BlockSpec(shape, functools.partial(index, s=s)) for s in range(self.n_split)]

    def slot_scratch(self):
        width = self.n_split * self.tn
        return pltpu.VMEM((2, width, self.k) if self.transposed else (2, self.k, width), BF16)


def _stage_weights(cfg, w_refs, wslot_ref):
    j, m = pl.program_id(0), pl.program_id(1)
    chunk = pl.ds(pl.multiple_of(m * cfg.chunk_rows, cfg.chunk_rows), cfg.chunk_rows)
    for s, w_ref in enumerate(w_refs):
        split = slice(s * cfg.tn, (s + 1) * cfg.tn)
        if cfg.transposed:
            wslot_ref[j % 2, split, chunk] = w_ref[...].astype(BF16)
        else:
            wslot_ref[j % 2, chunk, split] = w_ref[...].astype(BF16)


def _slot_product(cfg, a, wslot_ref, slot, s):
    split = slice(s * cfg.tn, (s + 1) * cfg.tn)
    if cfg.transposed:
        return _dot_nt(a, wslot_ref[slot, split, :])
    return _dot(a, wslot_ref[slot, :, split])


def _staged_body(cfg, epilogue, first_step, a_ref, *refs):
    w_refs, rest = refs[:cfg.n_split], refs[cfg.n_split:]
    wslot_ref = rest[-1]
    j = pl.program_id(0)

    @pl.when(j == 0)
    def _():
        _stage_weights(cfg, w_refs, wslot_ref)
        if first_step is not None:
            first_step(*rest[:-1])

    @pl.when(j > 0)
    def _():
        _stage_weights(cfg, w_refs, wslot_ref)
        a = a_ref[...]
        slot = (j + 1) % 2
        epilogue(lambda s: _slot_product(cfg, a, wslot_ref, slot, s), *rest[:-1])


def _staged_call(cfg, epilogue, first_step, a, w, extra_in, extra_specs, out_specs, out_shape,
                 scratch, vmem_mib, name):
    return pl.pallas_call(
        functools.partial(_staged_body, cfg, epilogue, first_step),
        grid=cfg.grid,
        in_specs=[cfg.act_spec()] + cfg.weight_specs() + extra_specs,
        out_specs=out_specs,
        out_shape=out_shape,
        scratch_shapes=scratch + [cfg.slot_scratch()],
        compiler_params=_params(("arbitrary", "arbitrary"), vmem_mib),
        name=name,
    )(a, *([w] * cfg.n_split), *extra_in)


def _proj_tiling(n_split, transposed=False):
    return _Tiling(D_MODEL, PROJ_TM, PROJ_TN, n_split, D_INNER, transposed)


def _epilogue_a(z, ug_ref, v_ref):
    sg = _silu(z(2))
    v_ref[...] = z(1).astype(v_ref.dtype)
    ug_ref[...] = (z(0) * sg).astype(ug_ref.dtype)


def _proj_a(h, w_in):
    cfg = _proj_tiling(3)
    return _staged_call(
        cfg, _epilogue_a, None, h, w_in, [], [],
        [cfg.row_tile_spec(), cfg.row_tile_spec()],
        [jax.ShapeDtypeStruct((SEQ, D_INNER), BF16), jax.ShapeDtypeStruct((SEQ, D_INNER), BF16)],
        [], 52, "proj_a")


def _epilogue_b(z, qt_ref, k_ref, vt_ref, sg_ref):
    qt_ref[...] = (z(0) * (HEAD_DIM ** -0.5 * LOG2E)).T.astype(BF16)
    vt_ref[...] = z(2).T.astype(BF16)
    sg_ref[...] = _silu(z(3))
    k_ref[...] = z(1).astype(BF16)


def _proj_b(h, w_in_t):
    cfg = _proj_tiling(4, transposed=True)
    return _staged_call(
        cfg, _epilogue_b, None, h, w_in_t, [], [],
        [cfg.col_tile_spec(), cfg.row_tile_spec(), cfg.col_tile_spec(), cfg.row_tile_spec()],
        [jax.ShapeDtypeStruct((D_INNER, SEQ), BF16), jax.ShapeDtypeStruct((SEQ, D_INNER), BF16),
         jax.ShapeDtypeStruct((D_INNER, SEQ), BF16), jax.ShapeDtypeStruct((SEQ, D_INNER), F32)],
        [], 56, "proj_b")


def _first_step_c(cw_ref, y_ref, carry_ref):
    carry_ref[...] = jnp.zeros_like(carry_ref)


def _epilogue_c(z, cw_ref, y_ref, carry_ref):
    m = pl.program_id(1)
    inner = z(1) * z(2)
    prev = jnp.where(m == 0, 0.0, carry_ref[...])
    p1 = prev[SUBLANES - 1:SUBLANES, :]
    p2 = prev[SUBLANES - 2:SUBLANES - 1, :]
    row = lax.broadcasted_iota(jnp.int32, inner.shape, 0)
    d1 = jnp.where(row == 0, p1, pltpu.roll(inner, 1, 0))
    d2 = jnp.where(row == 0, p2, jnp.where(row == 1, p1, pltpu.roll(inner, 2, 0)))
    cw = cw_ref[...]
    conv = cw[0:1, :] * d2 + cw[1:2, :] * d1 + cw[2:3, :] * inner
    carry_ref[...] = inner[PROJ_TM - SUBLANES:, :]
    gated = conv * _silu(z(3))
    y_ref[...] = (z(0) * gated).astype(y_ref.dtype)


def _proj_c(h, w_in, conv_w):
    cfg = _proj_tiling(4)
    return _staged_call(
        cfg, _epilogue_c, _first_step_c, h, w_in, [conv_w],
        [pl.BlockSpec((CONV_WIDTH, PROJ_TN), lambda j, m: (0, cfg.col_of(j)))],
        cfg.row_tile_spec(), jax.ShapeDtypeStruct((SEQ, D_INNER), BF16),
        [pltpu.VMEM((SUBLANES, PROJ_TN), F32)], 56, "proj_c")


def _epilogue_out(z, x_ref, o_ref):
    o_ref[...] = x_ref[...] + z(0)


def _out_proj(y, w_out, x):
    cfg = _Tiling(D_INNER, OUT_TM, OUT_TN, 1, D_MODEL, False)
    return _staged_call(
        cfg, _epilogue_out, None, y, w_out, [x], [cfg.row_tile_spec()],
        cfg.row_tile_spec(), jax.ShapeDtypeStruct((SEQ, D_MODEL), F32),
        [], 48, "out_proj")


def _sgu_kernel(v_ref, ug_ref, gain_ref, ws_ref, bias_ref, y_ref, wsc_ref):
    n_groups = ws_ref.shape[0]

    @pl.when(pl.program_id(0) == 0)
    def _():
        t = lax.broadcasted_iota(jnp.int32, (CHUNK, CHUNK), 0)
        s = lax.broadcasted_iota(jnp.int32, (CHUNK, CHUNK), 1)
        for g in range(n_groups):
            wsc_ref[g] = jnp.where(t >= s, ws_ref[g], 0.0).astype(BF16)

    groups = [slice(g * GROUP_WIDTH, (g + 1) * GROUP_WIDTH) for g in range(n_groups)]
    chunks = [slice(c * CHUNK, (c + 1) * CHUNK) for c in range(SGU_CHUNKS)]
    squares = jnp.zeros((SGU_CHUNKS * CHUNK, GROUP_WIDTH), F32)
    for cols in groups:
        x = v_ref[:, cols].astype(F32)
        squares = squares + x * x
    r = lax.rsqrt(jnp.sum(squares, axis=-1, keepdims=True) * (1.0 / v_ref.shape[1]) + RMS_EPS)
    for g, cols in enumerate(groups):
        vn = (v_ref[:, cols].astype(F32) * r * gain_ref[:, cols]).astype(BF16)
        s = _dot(wsc_ref[g], jnp.concatenate([vn[rows] for rows in chunks], axis=1))
        for c, rows in enumerate(chunks):
            s_c = s[:, c * GROUP_WIDTH:(c + 1) * GROUP_WIDTH] + bias_ref[g]
            y_ref[rows, cols] = (ug_ref[rows, cols].astype(F32) * s_c).astype(y_ref.dtype)


def _sgu(v, ug, v_gain, ws, ws_bias):
    n_groups = D_INNER // GROUP_WIDTH
    rows = SGU_CHUNKS * CHUNK
    bias_b = jnp.broadcast_to(ws_bias[:, :, None], (n_groups, CHUNK, GROUP_WIDTH))
    return pl.pallas_call(
        _sgu_kernel,
        grid=(SEQ // rows,),
        in_specs=[pl.BlockSpec((rows, D_INNER), lambda n: (n, 0)),
                  pl.BlockSpec((rows, D_INNER), lambda n: (n, 0)),
                  pl.BlockSpec((1, D_INNER), lambda n: (0, 0)),
                  pl.BlockSpec((n_groups, CHUNK, CHUNK), lambda n: (0, 0, 0)),
                  pl.BlockSpec((n_groups, CHUNK, GROUP_WIDTH), lambda n: (0, 0, 0))],
        out_specs=pl.BlockSpec((rows, D_INNER), lambda n: (n, 0)),
        out_shape=jax.ShapeDtypeStruct((SEQ, D_INNER), BF16),
        scratch_shapes=[pltpu.VMEM((n_groups, CHUNK, CHUNK), BF16)],
        compiler_params=_params(("arbitrary",), 48),
        name="sgu",
    )(v, ug, v_gain.reshape(1, D_INNER), ws, bias_b)


DECAY_GROUP = 32
assert SPLIT * DECAY_GROUP <= LANES - SPLIT


def _split_bf16(x):
    pieces, rest = [], x
    for _ in range(SPLIT):
        piece = rest.astype(BF16).astype(F32)
        pieces.append(piece)
        rest = rest - piece
    return pieces


def _forget_cumsum_kernel(h_ref, wf_ref, bf_ref, kc_ref, qc_ref, carry_ref, wpad_ref):
    n_heads = qc_ref.shape[0]

    @pl.when(pl.program_id(0) == 0)
    def _():
        carry_ref[...] = jnp.zeros_like(carry_ref)
        wpad_ref[:n_heads, :] = wf_ref[...].astype(BF16)
        wpad_ref[n_heads:, :] = jnp.zeros((LANES - n_heads, wpad_ref.shape[1]), BF16)

    x = _dot_nt(h_ref[...], wpad_ref[...]) + bf_ref[...]
    log_f = jnp.minimum(x, 0.0) - jnp.log1p(jnp.exp(-jnp.abs(x)))
    t = lax.broadcasted_iota(jnp.int32, (CUM_ROWS, CUM_ROWS), 0)
    s = lax.broadcasted_iota(jnp.int32, (CUM_ROWS, CUM_ROWS), 1)
    tril = (t >= s).astype(F32)
    cum = jnp.dot(tril, log_f, preferred_element_type=F32,
                  precision=lax.Precision.HIGHEST) + carry_ref[...]
    carry_ref[...] = cum[CUM_ROWS - 1:CUM_ROWS, :]
    c2 = cum * LOG2E
    pieces = _split_bf16(c2)
    head_row = lax.broadcasted_iota(jnp.int32, (LANES, LANES), 0)
    lane_col = lax.broadcasted_iota(jnp.int32, (LANES, LANES), 1)
    lane = lax.broadcasted_iota(jnp.int32, (CUM_ROWS, LANES), 1)
    for g in range(kc_ref.shape[0]):
        local = head_row - g * DECAY_GROUP
        in_group = (local >= 0) & (local < DECAY_GROUP)
        tile = jnp.zeros((CUM_ROWS, LANES), F32)
        for i, piece in enumerate(pieces):
            select = jnp.where(in_group & (lane_col == SPLIT * local + i), 1.0, 0.0).astype(BF16)
            tile = tile + _dot(piece.astype(BF16), select)
        kc_ref[g] = jnp.where(lane >= LANES - SPLIT, 1.0, tile).astype(BF16)
    sub = lax.broadcasted_iota(jnp.int32, (BF16_ROWS, CUM_ROWS), 0)
    pieces_t = [piece.T for piece in pieces]
    for hd in range(n_heads):
        q_tile = jnp.zeros((BF16_ROWS, CUM_ROWS), F32)
        for i, piece_t in enumerate(pieces_t):
            q_tile = jnp.where(sub == BF16_ROWS - SPLIT + i, piece_t[hd:hd + 1, :], q_tile)
        qc_ref[hd] = q_tile.astype(BF16)


def _forget_cumsum(h, w_in_t, b_f):
    n_heads = D_INNER // HEAD_DIM
    n_groups = pl.cdiv(n_heads, DECAY_GROUP)
    assert w_in_t.shape[0] == 4 * D_INNER + n_heads and (4 * D_INNER) % n_heads == 0
    return pl.pallas_call(
        _forget_cumsum_kernel,
        grid=(SEQ // CUM_ROWS,),
        in_specs=[pl.BlockSpec((CUM_ROWS, D_MODEL), lambda i: (i, 0)),
                  pl.BlockSpec((n_heads, D_MODEL), lambda i: (4 * D_INNER // n_heads, 0)),
                  pl.BlockSpec((1, LANES), lambda i: (0, 0))],
        out_specs=[pl.BlockSpec((n_groups, CUM_ROWS, LANES), lambda i: (0, i, 0)),
                   pl.BlockSpec((n_heads, BF16_ROWS, CUM_ROWS), lambda i: (0, 0, i))],
        out_shape=[jax.ShapeDtypeStruct((n_groups, SEQ, LANES), BF16),
                   jax.ShapeDtypeStruct((n_heads, BF16_ROWS, SEQ), BF16)],
        scratch_shapes=[pltpu.VMEM((1, LANES), F32), pltpu.VMEM((LANES, D_MODEL), BF16)],
        compiler_params=_params(("arbitrary",), 48),
        name="forget_cumsum",
    )(h, w_in_t, b_f)


def _col_reduce(op, x):
    group = 8 * SUBLANES
    parts = [x[r:r + group] for r in range(0, x.shape[0], group)]
    acc = parts[0]
    for part in parts[1:]:
        acc = op(acc, part)
    reduce = jnp.max if op is jnp.maximum else jnp.sum
    return reduce(acc, axis=0, keepdims=True)


def _attn_kernel(qt_ref, qc_ref, k_ref, kc_ref, vt_ref, sg_ref, y_ref, m_ref, acc_ref, st0_ref):
    ip = pl.program_id(1)
    streams = [(hh, qs) for qs in range(ATT_QSUB) for hh in range(ATT_HEADS)]
    head = lambda hh: slice(hh * HEAD_DIM, (hh + 1) * HEAD_DIM)
    qcols = lambda qs: slice(qs * ATT_TQ, (qs + 1) * ATT_TQ)
    sub = lax.broadcasted_iota(jnp.int32, (LANES - BF16_ROWS, ATT_TQ), 0)
    q_aug = {}
    for hh in range(ATT_HEADS):
        first = SPLIT * ((pl.program_id(0) * ATT_HEADS + hh) % DECAY_GROUP)
        minus = jnp.where((sub >= first) & (sub < first + SPLIT), -1.0, 0.0).astype(BF16)
        for qs in range(ATT_QSUB):
            q_aug[(hh, qs)] = jnp.concatenate(
                [qt_ref[head(hh), qcols(qs)], minus, qc_ref[hh, :, qcols(qs)]], axis=0)

    def scores(j, stream):
        hh, _ = stream
        keys = pl.ds(pl.multiple_of(j * ATT_TK, ATT_TK), ATT_TK)
        k_aug = jnp.concatenate([k_ref[keys, head(hh)], kc_ref[keys, :]], axis=1)
        return _dot(k_aug, q_aug[stream])

    def accumulate(j, stream, alpha, p):
        hh, _ = stream
        n = streams.index(stream)
        keys = pl.ds(pl.multiple_of(j * ATT_TK, ATT_TK), ATT_TK)
        v_aug = jnp.concatenate([vt_ref[head(hh), keys], jnp.ones((BF16_ROWS, ATT_TK), BF16)], axis=0)
        acc_ref[n] = alpha * acc_ref[n] + _dot(v_aug, p)

    def absorb(j, stream, st, masked):
        _, qs = stream
        n = streams.index(stream)
        if masked:
            kpos = j * ATT_TK + lax.broadcasted_iota(jnp.int32, (ATT_TK, ATT_TQ), 0)
            qpos = (ip * ATT_QSUB + qs) * ATT_TQ + lax.broadcasted_iota(jnp.int32, (ATT_TK, ATT_TQ), 1)
            st = jnp.where(qpos >= kpos, st, NEG_INF)
        m = m_ref[n]
        m_new = jnp.maximum(m, _col_reduce(jnp.maximum, st))
        alpha = jnp.exp2(m - m_new)
        p = jnp.exp2((st - m_new).astype(BF16))
        m_ref[n] = m_new
        accumulate(j, stream, alpha, p)

    def step(j, active, masked, rotated, prefetch):
        st = {active[0]: st0_ref[...] if rotated else scores(j, active[0])}
        for n, stream in enumerate(active):
            if n + 1 < len(active):
                st[active[n + 1]] = scores(j, active[n + 1])
            elif prefetch:
                st0_ref[...] = scores(j + 1, streams[0])
            absorb(j, stream, st.pop(stream), stream in masked)

    m_ref[...] = jnp.full(m_ref.shape, NEG_INF, F32)
    acc_ref[...] = jnp.zeros(acc_ref.shape, F32)
    st0_ref[...] = scores(0, streams[0])

    @pl.loop(0, ip)
    def _(jj):
        for u in range(ATT_QSUB):
            step(jj * ATT_QSUB + u, streams, (), True, True)

    for qd in range(ATT_QSUB):
        active = [stream for stream in streams if stream[1] >= qd]
        step(ip * ATT_QSUB + qd, active, [stream for stream in active if stream[1] == qd], qd == 0, False)
    for n, (hh, qs) in enumerate(streams):
        o = (acc_ref[n, :HEAD_DIM, :] / acc_ref[n, HEAD_DIM:HEAD_DIM + 1, :]).T
        y_ref[qcols(qs), head(hh)] = (o * sg_ref[qcols(qs), head(hh)]).astype(y_ref.dtype)


def _attention(qt, qc, k, kc, vt, sg):
    width = ATT_HEADS * HEAD_DIM
    rows = ATT_QSUB * ATT_TQ
    n_streams = ATT_HEADS * ATT_QSUB
    assert DECAY_GROUP % ATT_HEADS == 0
    return pl.pallas_call(
        _attn_kernel,
        grid=(D_INNER // width, SEQ // rows),
        in_specs=[pl.BlockSpec((width, rows), lambda h, i: (h, i)),
                  pl.BlockSpec((ATT_HEADS, BF16_ROWS, rows), lambda h, i: (h, 0, i)),
                  pl.BlockSpec((SEQ, width), lambda h, i: (0, h)),
                  pl.BlockSpec((None, SEQ, LANES), lambda h, i: (h * ATT_HEADS // DECAY_GROUP, 0, 0)),
                  pl.BlockSpec((width, SEQ), lambda h, i: (h, 0)),
                  pl.BlockSpec((rows, width), lambda h, i: (i, h))],
        out_specs=pl.BlockSpec((rows, width), lambda h, i: (i, h)),
        out_shape=jax.ShapeDtypeStruct((SEQ, D_INNER), BF16),
        scratch_shapes=[pltpu.VMEM((n_streams, 1, ATT_TQ), F32),
                        pltpu.VMEM((n_streams, HEAD_DIM + BF16_ROWS, ATT_TQ), F32),
                        pltpu.VMEM((ATT_TK, ATT_TQ), F32)],
        compiler_params=_params(("parallel", "arbitrary"), 56),
        name="fox_attention",
    )(qt, qc, k, kc, vt, sg)


def _layer_a(x, norm_g, w_in, v_gain, ws, ws_bias, w_out):
    h = _rmsnorm(x, norm_g, BF16)
    ug, v = _proj_a(h, w_in)
    y = _sgu(v, ug, v_gain, ws, ws_bias)
    return _out_proj(y, w_out, x)


def _layer_b(x, norm_g, w_in, b_f, w_out):
    h = _rmsnorm(x, norm_g, BF16)
    n_heads = D_INNER // HEAD_DIM
    pad = LANES - n_heads
    w_in_t = w_in.T
    b_pad = jnp.pad(b_f, (0, pad)).reshape(1, LANES)
    qt, k, vt, sg = _proj_b(h, w_in_t)
    kc, qc = _forget_cumsum(h, w_in_t, b_pad)
    y = _attention(qt, qc, k, kc, vt, sg)
    return _out_proj(y, w_out, x)


def _layer_c(x, norm_g, w_in, conv_w, w_out):
    h = _rmsnorm(x, norm_g, BF16)
    y = _proj_c(h, w_in, conv_w)
    return _out_proj(y, w_out, x)


def kernel(x, l0_norm, l0_w_in, l0_v_gain, l0_ws, l0_ws_bias, l0_w_out, l1_norm, l1_w_in, l1_b_f, l1_w_out, l2_norm, l2_w_in, l2_conv_w, l2_w_out, l3_norm, l3_w_in, l3_v_gain, l3_ws, l3_ws_bias, l3_w_out, final_norm):
    batch, seq, d_model = x.shape
    assert (batch, seq, d_model) == (1, SEQ, D_MODEL)
    xs = x.reshape(SEQ, D_MODEL)
    xs = _layer_a(xs, l0_norm, l0_w_in, l0_v_gain, l0_ws, l0_ws_bias, l0_w_out)
    xs = _layer_b(xs, l1_norm, l1_w_in, l1_b_f, l1_w_out)
    xs = _layer_c(xs, l2_norm, l2_w_in, l2_conv_w, l2_w_out)
    xs = _layer_a(xs, l3_norm, l3_w_in, l3_v_gain, l3_ws, l3_ws_bias, l3_w_out)
    return _rmsnorm(xs, final_norm, F32).reshape(batch, seq, d_model)
```
